```python
import jax, jax.numpy as jnp
from jax import lax
import numpy as np

D_MODEL = 2048
BATCH = 4
SEQ = 2048
DEPTH = 2
DEC_BATCH = 128
DEC_SEQ = 4
PAST_LEN = 16384
PAGE_SIZE = 128

D_CONV = 1024
CONV_W = 31
D_GMLP = 1024
GMLP_HEADS = 8
GMLP_HD = D_GMLP // GMLP_HEADS
CHUNK = 128
D_POOL = 1024
POOL_WINDOWS = (2, 4, 8, 16)
POOL_GROUPS = len(POOL_WINDOWS)
POOL_GD = D_POOL // POOL_GROUPS
POOL_BUF = max(POOL_WINDOWS) - 1
N_BRANCH = 3
D_IN = 2 * D_CONV + 2 * D_GMLP + D_POOL + N_BRANCH * D_MODEL
SPLITS = (2 * D_CONV, 2 * D_CONV + 2 * D_GMLP, 2 * D_CONV + 2 * D_GMLP + D_POOL)
N_EXPERTS = 64
TOP_K = 8
N_GROUPS = 8
TOPK_GROUPS = 4
D_EXPERT = 512
D_SHARED = 512
ROUTE_SCALE = 2.5
MOE_BLOCK = 128
EPS = 1e-6

kernel_name = 'hybrid_conv_gmlp_pool_moe_decoder_step'


def rmsnorm(x, g):
    xf = x.astype(jnp.float32)
    y = xf * lax.rsqrt(jnp.mean(xf * xf, axis=-1, keepdims=True) + EPS)
    return (y * g.astype(jnp.float32)).astype(x.dtype)


def layernorm(x, g, b):
    xf = x.astype(jnp.float32)
    mu = jnp.mean(xf, axis=-1, keepdims=True)
    var = jnp.mean(jnp.square(xf - mu), axis=-1, keepdims=True)
    y = (xf - mu) * lax.rsqrt(var + EPS) * g.astype(jnp.float32) + b.astype(jnp.float32)
    return y.astype(x.dtype)


def conv_branch(z, conv_prev, w_dw, b_dw, ln_g, ln_b, w_pw):
    a, gt = jnp.split(z, 2, axis=-1)
    glu = a * jax.nn.sigmoid(gt)
    xpad = jnp.concatenate([conv_prev.astype(glu.dtype), glu], axis=1)
    y = lax.conv_general_dilated(xpad, w_dw[:, None, :], window_strides=(1,), padding='VALID',
                                 dimension_numbers=('NWC', 'WIO', 'NWC'),
                                 feature_group_count=D_CONV) + b_dw
    y = jax.nn.silu(layernorm(y, ln_g, ln_b))
    return y @ w_pw, xpad[:, -(CONV_W - 1):]


def gmlp_branch(z, ln_g, ln_b, w_s, b_s, w_o):
    zb = jax.nn.gelu(z, approximate=False)
    u, v = jnp.split(zb, 2, axis=-1)
    v = layernorm(v, ln_g, ln_b)
    B, L, _ = v.shape
    n_chunks = -(-L // CHUNK)
    Lp = n_chunks * CHUNK
    vp = jnp.pad(v, ((0, 0), (0, Lp - L), (0, 0))).reshape(B, n_chunks, CHUNK, GMLP_HEADS, GMLP_HD)
    causal = jnp.tril(jnp.ones((CHUNK, CHUNK), dtype=bool))
    ws = jnp.where(causal[None], w_s, 0)
    mixed = jnp.einsum('hts,bnshd->bnthd', ws, vp) + b_s.T[:, :, None]
    mixed = mixed.reshape(B, Lp, D_GMLP)[:, :L]
    return (u * mixed) @ w_o, v


def pool_branch(p, pool_prev, n_prev, w_pool, pool_scale):
    B, L, _ = p.shape
    xcat = jnp.concatenate([pool_prev.astype(p.dtype), p], axis=1)
    xf = xcat.astype(jnp.float32)
    cs = jnp.concatenate([jnp.zeros((B, 1, D_POOL), jnp.float32), jnp.cumsum(xf, axis=1)], axis=1)
    end = cs[:, POOL_BUF + 1:]
    tok = xf[:, POOL_BUF:]
    t = jnp.arange(L)
    outs = []
    for g, w in enumerate(POOL_WINDOWS):
        lo, hi = g * POOL_GD, (g + 1) * POOL_GD
        start = cs[:, POOL_BUF + 1 - w: POOL_BUF + 1 - w + L, lo:hi]
        cnt = jnp.minimum(t + n_prev + 1, w).astype(jnp.float32)[None, :, None]
        outs.append((end[..., lo:hi] - start) / cnt - tok[..., lo:hi])
    pooled = jnp.stack(outs, axis=2).astype(p.dtype)
    y = jnp.einsum('blgc,gce->blge', pooled, w_pool).reshape(B, L, D_MODEL) * pool_scale
    return y, xcat[:, -POOL_BUF:]


def route(xf, w_router, router_bias):
    scores = jax.nn.sigmoid(xf.astype(jnp.float32) @ w_router.astype(jnp.float32))
    biased = scores + router_bias.astype(jnp.float32)
    T = xf.shape[0]
    grp = biased.reshape(T, N_GROUPS, N_EXPERTS // N_GROUPS)
    grp_score = lax.top_k(grp, 2)[0].sum(-1)
    _, top_g = lax.top_k(grp_score, TOPK_GROUPS)
    gmask = jnp.any(top_g[:, :, None] == jnp.arange(N_GROUPS)[None, None, :], axis=1)
    emask = jnp.repeat(gmask, N_EXPERTS // N_GROUPS, axis=1)
    _, idx = lax.top_k(jnp.where(emask, biased, -jnp.inf), TOP_K)
    w = jnp.take_along_axis(scores, idx, axis=1)
    w = w / jnp.sum(w, axis=-1, keepdims=True) * ROUTE_SCALE
    return idx, w


def routed_experts(xf, idx, wts, w1, w3, w2):
    T, D = xf.shape
    A = T * TOP_K
    n_blocks = -(-(A + N_EXPERTS * (MOE_BLOCK - 1)) // MOE_BLOCK)
    flat_e = idx.reshape(A)
    order = jnp.argsort(flat_e)
    e_sorted = flat_e[order]
    tok_sorted = order // TOP_K
    w_sorted = wts.reshape(A)[order].astype(xf.dtype)
    counts = jnp.bincount(flat_e, length=N_EXPERTS)
    padded = (counts + MOE_BLOCK - 1) // MOE_BLOCK * MOE_BLOCK
    pad_end = jnp.cumsum(padded)
    pad_start = pad_end - padded
    start = jnp.cumsum(counts) - counts
    dest = pad_start[e_sorted] + jnp.arange(A) - start[e_sorted]
    buf_tok = jnp.zeros((n_blocks * MOE_BLOCK,), jnp.int32).at[dest].set(tok_sorted)
    blk_e = jnp.minimum(jnp.searchsorted(pad_end, jnp.arange(n_blocks) * MOE_BLOCK, side='right'),
                        N_EXPERTS - 1)
    xb = xf[buf_tok].reshape(n_blocks, MOE_BLOCK, D)

    def expert_block(args):
        xblk, e = args
        hid = jax.nn.silu(xblk @ w1[e]) * (xblk @ w3[e])
        return hid @ w2[e]

    yb = lax.map(expert_block, (xb, blk_e)).reshape(n_blocks * MOE_BLOCK, D)
    return jnp.zeros_like(xf).at[tok_sorted].add(yb[dest] * w_sorted[:, None])


def moe_ffn(x, w_router, router_bias, w1, w3, w2, ws1, ws3, ws2):
    B, L, D = x.shape
    xf = x.reshape(B * L, D)
    idx, wts = route(xf, w_router, router_bias)
    routed = routed_experts(xf, idx, wts, w1, w3, w2)
    shared = (jax.nn.silu(xf @ ws1) * (xf @ ws3)) @ ws2
    return (routed + shared).reshape(B, L, D)


def trunk_layer(x, c, conv_prev, pool_prev, n_prev, prm):
    B, L, _ = x.shape
    ada = (jax.nn.silu(c) @ prm['w_ada'] + prm['b_ada'])[:, None, :]
    sh1, sc1, g1, sh2, sc2, g2 = jnp.split(ada, 6, axis=-1)
    h = rmsnorm(x, prm['g_pre_mix']) * (1 + sc1) + sh1
    z = h @ prm['w_in']
    z_conv, z_gmlp, z_pool, z_gate = jnp.split(z, SPLITS, axis=-1)
    y_a, conv_new = conv_branch(z_conv, conv_prev, prm['conv_w'], prm['conv_b'],
                                prm['conv_ln_g'], prm['conv_ln_b'], prm['conv_w_pw'])
    y_b, v_rows = gmlp_branch(z_gmlp, prm['gmlp_ln_g'], prm['gmlp_ln_b'],
                              prm['gmlp_w_s'], prm['gmlp_b_s'], prm['gmlp_w_o'])
    y_c, pool_new = pool_branch(z_pool, pool_prev, n_prev, prm['pool_w'], prm['pool_scale'])
    gates = jax.nn.sigmoid(z_gate).reshape(B, L, N_BRANCH, D_MODEL)
    merged = gates[:, :, 0] * y_a + gates[:, :, 1] * y_b + gates[:, :, 2] * y_c
    mix = merged @ prm['w_out']
    x = x + g1 * rmsnorm(mix, prm['g_post_mix'])
    h2 = rmsnorm(x, prm['g_pre_ffn']) * (1 + sc2) + sh2
    f = moe_ffn(h2, prm['router_w'], prm['router_bias'], prm['w1'], prm['w3'], prm['w2'],
                prm['ws1'], prm['ws3'], prm['ws2'])
    x = x + g2 * rmsnorm(f, prm['g_post_ffn'])
    return x, conv_new, pool_new, v_rows


def setup_inputs(seed: int = 0) -> dict:
    key = jax.random.key(seed)
    ks = iter(jax.random.split(key, 40))
    nrm = lambda shape, s: jax.random.normal(next(ks), shape, jnp.float32) * s
    gain = lambda shape: 1.0 + nrm(shape, 0.02)
    return {
        'x_prompt': nrm((BATCH, SEQ, D_MODEL), 1.0),
        'x_sample': nrm((DEC_BATCH, DEC_SEQ, D_MODEL), 1.0),
        'state_conv': nrm((DEPTH, DEC_BATCH, CONV_W - 1, D_CONV), 0.5),
        'state_pool': nrm((DEPTH, DEC_BATCH, POOL_BUF, D_POOL), 1.0),
        'c_prompt': nrm((BATCH, D_MODEL), 1.0),
        'c_sample': nrm((DEC_BATCH, D_MODEL), 1.0),
        'w_ada': nrm((DEPTH, D_MODEL, 6 * D_MODEL), 0.5 * D_MODEL ** -0.5),
        'b_ada': nrm((DEPTH, 6 * D_MODEL), 0.02),
        'g_pre_mix': gain((DEPTH, D_MODEL)),
        'g_post_mix': gain((DEPTH, D_MODEL)),
        'g_pre_ffn': gain((DEPTH, D_MODEL)),
        'g_post_ffn': gain((DEPTH, D_MODEL)),
        'w_in': nrm((DEPTH, D_MODEL, D_IN), D_MODEL ** -0.5),
        'conv_w': nrm((DEPTH, CONV_W, D_CONV), CONV_W ** -0.5),
        'conv_b': nrm((DEPTH, D_CONV), 0.02),
        'conv_ln_g': gain((DEPTH, D_CONV)),
        'conv_ln_b': nrm((DEPTH, D_CONV), 0.02),
        'conv_w_pw': nrm((DEPTH, D_CONV, D_MODEL), D_CONV ** -0.5),
        'gmlp_ln_g': gain((DEPTH, D_GMLP)),
        'gmlp_ln_b': nrm((DEPTH, D_GMLP), 0.02),
        'gmlp_w_s': nrm((DEPTH, GMLP_HEADS, CHUNK, CHUNK), CHUNK ** -0.5),
        'gmlp_b_s': gain((DEPTH, GMLP_HEADS, CHUNK)),
        'gmlp_w_o': nrm((DEPTH, D_GMLP, D_MODEL), D_GMLP ** -0.5),
        'pool_w': nrm((DEPTH, POOL_GROUPS, POOL_GD, D_MODEL // POOL_GROUPS), POOL_GD ** -0.5),
        'pool_scale': gain((DEPTH, D_MODEL)),
        'w_out': nrm((DEPTH, D_MODEL, D_MODEL), D_MODEL ** -0.5),
        'router_w': nrm((DEPTH, D_MODEL, N_EXPERTS), D_MODEL ** -0.5),
        'router_bias': nrm((DEPTH, N_EXPERTS), 0.01),
        'w1': nrm((DEPTH, N_EXPERTS, D_MODEL, D_EXPERT), D_MODEL ** -0.5),
        'w3': nrm((DEPTH, N_EXPERTS, D_MODEL, D_EXPERT), D_MODEL ** -0.5),
        'w2': nrm((DEPTH, N_EXPERTS, D_EXPERT, D_MODEL), D_EXPERT ** -0.5),
        'ws1': nrm((DEPTH, D_MODEL, D_SHARED), D_MODEL ** -0.5),
        'ws3': nrm((DEPTH, D_MODEL, D_SHARED), D_MODEL ** -0.5),
        'ws2': nrm((DEPTH, D_SHARED, D_MODEL), D_SHARED ** -0.5),
    }


def reference(x_prompt, x_sample, state_conv, state_pool, c_prompt, c_sample,
              w_ada, b_ada, g_pre_mix, g_post_mix, g_pre_ffn, g_post_ffn, w_in,
              conv_w, conv_b, conv_ln_g, conv_ln_b, conv_w_pw,
              gmlp_ln_g, gmlp_ln_b, gmlp_w_s, gmlp_b_s, gmlp_w_o,
              pool_w, pool_scale, w_out, router_w, router_bias, w1, w3, w2, ws1, ws3, ws2):
    xp, xs = x_prompt, x_sample
    conv_zero = jnp.zeros((xp.shape[0], CONV_W - 1, D_CONV), xp.dtype)
    pool_zero = jnp.zeros((xp.shape[0], POOL_BUF, D_POOL), xp.dtype)
    n_prev_sample = min(PAST_LEN, POOL_BUF)
    conv_p, pool_p, conv_s, pool_s, v_s = [], [], [], [], []
    for l in range(DEPTH):
        prm = {
            'w_ada': w_ada[l], 'b_ada': b_ada[l],
            'g_pre_mix': g_pre_mix[l], 'g_post_mix': g_post_mix[l],
            'g_pre_ffn': g_pre_ffn[l], 'g_post_ffn': g_post_ffn[l],
            'w_in': w_in[l],
            'conv_w': conv_w[l], 'conv_b': conv_b[l], 'conv_ln_g': conv_ln_g[l],
            'conv_ln_b': conv_ln_b[l], 'conv_w_pw': conv_w_pw[l],
            'gmlp_ln_g': gmlp_ln_g[l], 'gmlp_ln_b': gmlp_ln_b[l], 'gmlp_w_s': gmlp_w_s[l],
            'gmlp_b_s': gmlp_b_s[l], 'gmlp_w_o': gmlp_w_o[l],
            'pool_w': pool_w[l], 'pool_scale': pool_scale[l], 'w_out': w_out[l],
            'router_w': router_w[l], 'router_bias': router_bias[l],
            'w1': w1[l], 'w3': w3[l], 'w2': w2[l], 'ws1': ws1[l], 'ws3': ws3[l], 'ws2': ws2[l],
        }
        xp, cp, pp, _ = trunk_layer(xp, c_prompt, conv_zero, pool_zero, 0, prm)
        xs, cs_new, ps_new, vs = trunk_layer(xs, c_sample, state_conv[l], state_pool[l], n_prev_sample, prm)
        conv_p.append(cp)
        pool_p.append(pp)
        conv_s.append(cs_new)
        pool_s.append(ps_new)
        v_s.append(vs)
    conv_prompt = jnp.stack(conv_p)
    pool_prompt = jnp.stack(pool_p)
    conv_sample = jnp.stack(conv_s)
    pool_sample = jnp.stack(pool_s)
    gmlp_v_sample = jnp.stack(v_s)
    return (xp, xs, conv_prompt, pool_prompt, conv_sample, pool_sample, gmlp_v_sample)
```

```python
import functools

import jax
import jax.numpy as jnp
from jax import lax
from jax.experimental import pallas as pl
from jax.experimental.pallas import tpu as pltpu

F32 = jnp.float32
BF16 = jnp.bfloat16
I32 = jnp.int32

D = 2048
NP, LP = 4, 2048
NS, LS = 128, 4
RP = NP * LP
RS = NS * LS
R = RP + RS
DEPTH = 2
DC = 1024
CONV_W = 31
HIST_C = CONV_W - 1
HIST_P = 15
POOL_WINDOWS = (2, 4, 8, 16)
HEADS = 8
CHUNK = 128
D_IN = 5 * DC + 3 * D
COL_GMLP, COL_POOL, COL_GATE = 2 * DC, 4 * DC, 5 * DC
NE, TOPK, NGRP, TOPG = 64, 8, 8, 4
DE = 512
ROUTE_SCALE = 2.5
EPS = 1e-6

TM = 512
NT = R // TM
NTP = RP // TM
SUB = 128
NSUB = TM // SUB
TMM = 256
TK = 128
NTK = R // TK
NTKP = RP // TK
LANE = 128
MXU_N = 256
ROWC = D // LANE
BLK = 256
NASSIGN = R * TOPK
NB = -(-(NASSIGN + NE * (BLK - 1)) // BLK)
NROWS = NB * BLK

VMEM_LIMIT = 56 * 1024 * 1024


def _cp(*sem):
    return pltpu.CompilerParams(dimension_semantics=sem, vmem_limit_bytes=VMEM_LIMIT)


def _const_spec(shape):
    nd = len(shape)
    return pl.BlockSpec(shape, lambda *_: (0,) * nd, pipeline_mode=pl.Buffered(1))


def _rms(x):
    return x * lax.rsqrt(jnp.mean(x * x, axis=-1, keepdims=True) + EPS)


def _ln(x, g, b):
    mu = jnp.mean(x, axis=-1, keepdims=True)
    xc = x - mu
    var = jnp.mean(xc * xc, axis=-1, keepdims=True)
    return xc * lax.rsqrt(var + EPS) * g + b


def _silu(x):
    return x * jax.nn.sigmoid(x)


ADA_TN = 1024


def _ada_kernel(c_ref, w_ref, b_ref, o_ref):
    c = _silu(c_ref[...]).astype(BF16)
    o_ref[...] = jnp.dot(c, w_ref[...].astype(BF16), preferred_element_type=F32) + b_ref[...]


def _ada(c_all, w_ada, b_ada):
    n = c_all.shape[0]
    return pl.pallas_call(
        _ada_kernel,
        grid=(DEPTH, 6 * D // ADA_TN),
        in_specs=[
            pl.BlockSpec((n, D), lambda l, j: (0, 0)),
            pl.BlockSpec((None, D, ADA_TN), lambda l, j: (l, 0, j)),
            pl.BlockSpec((None, 1, ADA_TN), lambda l, j: (l, 0, j)),
        ],
        out_specs=pl.BlockSpec((None, n, ADA_TN), lambda l, j: (l, 0, j)),
        out_shape=jax.ShapeDtypeStruct((DEPTH, n, 6 * D), F32),
        compiler_params=_cp("arbitrary", "arbitrary"),
        name="ada",
    )(c_all, w_ada, b_ada.reshape(DEPTH, 1, 6 * D))


def _ada_specs(chunk, tile_rows):
    tiles_per_seq = LP // tile_rows
    p = pl.BlockSpec((None, 1, D), lambda i: (jnp.minimum(i // tiles_per_seq, NP - 1), 0, chunk))
    s = pl.BlockSpec((NS, D), lambda i: (0, chunk))
    return p, s


def _hmod_kernel(x_ref, g_ref, scp_ref, scs_ref, shp_ref, shs_ref, o_ref):
    is_p = pl.program_id(0) < NTP
    sc = 1.0 + jnp.where(is_p, scp_ref[...], scs_ref[...])
    sh = jnp.where(is_p, shp_ref[...], shs_ref[...])
    g = g_ref[...]
    for s in range(NSUB):
        rows = slice(s * SUB, (s + 1) * SUB)
        o_ref[rows, :] = (_rms(x_ref[rows, :]) * g * sc + sh).astype(o_ref.dtype)


def _hmod(x, g, ada_p, ada_s, c_scale, c_shift):
    scp, scs = _ada_specs(c_scale, TM)
    shp, shs = _ada_specs(c_shift, TM)
    return pl.pallas_call(
        _hmod_kernel,
        grid=(NT,),
        in_specs=[pl.BlockSpec((TM, D), lambda i: (i, 0)), _const_spec((1, D)), scp, scs, shp, shs],
        out_specs=pl.BlockSpec((TM, D), lambda i: (i, 0)),
        out_shape=jax.ShapeDtypeStruct((R, D), BF16),
        compiler_params=_cp("arbitrary"),
        name="hmod",
    )(x, g, ada_p, ada_s, ada_p, ada_s)


def _colmm_kernel(nw, epilogue, *refs):
    lhs_ref = refs[0]
    w_refs = refs[1:1 + nw]
    o_ref = refs[1 + nw]
    wb_refs = refs[2 + nw:]

    @pl.when(pl.program_id(1) == 0)
    def _():
        for w_ref, wb_ref in zip(w_refs, wb_refs):
            wb_ref[...] = w_ref[...].astype(BF16)

    lhs = lhs_ref[...]
    accs = [jnp.dot(lhs, wb_ref[...], preferred_element_type=F32) for wb_ref in wb_refs]
    o_ref[...] = epilogue(*accs).astype(o_ref.dtype)


def _colmm(h, w_in, layer, col_offsets, n_cols, tn, epilogue, out_dtype, name):
    nw = len(col_offsets)
    w_specs = [
        pl.BlockSpec((None, D, tn), functools.partial(lambda j, i, o: (layer, 0, o + j), o=off // tn))
        for off in col_offsets
    ]
    return pl.pallas_call(
        functools.partial(_colmm_kernel, nw, epilogue),
        grid=(n_cols // tn, NT),
        in_specs=[pl.BlockSpec((TM, D), lambda j, i: (i, 0))] + w_specs,
        out_specs=pl.BlockSpec((TM, tn), lambda j, i: (i, j)),
        out_shape=jax.ShapeDtypeStruct((R, n_cols), out_dtype),
        scratch_shapes=[pltpu.VMEM((D, tn), BF16) for _ in range(nw)],
        compiler_params=_cp("arbitrary", "arbitrary"),
        name=name,
    )(h, *([w_in] * nw))


def _glu_epilogue(a, g):
    return a * jax.nn.sigmoid(g)


def _gelu_epilogue(z):
    return 0.5 * z * (1.0 + lax.erf(z * (0.5 ** 0.5)))


def _id_epilogue(z):
    return z


def _sigmoid_epilogue(z):
    return jax.nn.sigmoid(z)


CONV_HALO = 32


def _conv_taps(w_ref, slab, cs):
    acc = None
    for k in range(CONV_W):
        term = w_ref[k:k + 1, cs] * slab(k)
        acc = term if acc is None else acc + term
    return acc


def _conv_p_kernel(main_ref, halo_ref, w_ref, b_ref, lg_ref, lb_ref, o_ref, xp_ref, y_ref):
    first = (pl.program_id(0) % (LP // SUB)) == 0
    xp_ref[0:CONV_HALO, :] = jnp.where(first, 0.0, halo_ref[...])
    xp_ref[CONV_HALO:, :] = main_ref[...]
    base = CONV_HALO - HIST_C
    for c in range(DC // LANE):
        cs = slice(c * LANE, (c + 1) * LANE)
        acc = _conv_taps(w_ref, lambda k: xp_ref[base + k:base + k + SUB, cs], cs)
        y_ref[:, cs] = acc + b_ref[:, cs]
    o_ref[...] = _silu(_ln(y_ref[...], lg_ref[...], lb_ref[...])).astype(o_ref.dtype)


def _conv_p(glu, w, b, lg, lb):
    halo_per_tile = SUB // CONV_HALO
    return pl.pallas_call(
        _conv_p_kernel,
        grid=(RP // SUB,),
        in_specs=[
            pl.BlockSpec((SUB, DC), lambda i: (i, 0)),
            pl.BlockSpec((CONV_HALO, DC), lambda i: (jnp.maximum(i * halo_per_tile - 1, 0), 0)),
            _const_spec((CONV_W, DC)), _const_spec((1, DC)), _const_spec((1, DC)), _const_spec((1, DC)),
        ],
        out_specs=pl.BlockSpec((SUB, DC), lambda i: (i, 0)),
        out_shape=jax.ShapeDtypeStruct((RP, DC), BF16),
        scratch_shapes=[pltpu.VMEM((CONV_HALO + SUB, DC), F32), pltpu.VMEM((SUB, DC), F32)],
        compiler_params=_cp("arbitrary"),
        name="conv_prompt",
    )(glu, glu, w, b, lg, lb)


def _conv_s_kernel(glu_ref, st_ref, w_ref, b_ref, lg_ref, lb_ref, o_ref, y_ref):
    for l in range(LS):
        rows = slice(l * SUB, (l + 1) * SUB)

        def slab(k, cs):
            j = l + k
            if j < HIST_C:
                return st_ref[j, :, cs]
            return glu_ref[(j - HIST_C) * SUB:(j - HIST_C + 1) * SUB, cs]

        for c in range(DC // LANE):
            cs = slice(c * LANE, (c + 1) * LANE)
            acc = _conv_taps(w_ref, functools.partial(slab, cs=cs), cs)
            y_ref[:, cs] = acc + b_ref[:, cs]
        o_ref[rows, :] = _silu(_ln(y_ref[...], lg_ref[...], lb_ref[...])).astype(o_ref.dtype)


def _conv_s(glu, state_t, w, b, lg, lb):
    return pl.pallas_call(
        _conv_s_kernel,
        grid=(1,),
        in_specs=[
            pl.BlockSpec((RS, DC), lambda i: (RP // RS, 0)),
            _const_spec((HIST_C, NS, DC)),
            _const_spec((CONV_W, DC)), _const_spec((1, DC)), _const_spec((1, DC)), _const_spec((1, DC)),
        ],
        out_specs=_const_spec((RS, DC)),
        out_shape=jax.ShapeDtypeStruct((RS, DC), BF16),
        scratch_shapes=[pltpu.VMEM((SUB, DC), F32)],
        compiler_params=_cp("arbitrary"),
        name="conv_sample",
    )(glu, state_t, w, b, lg, lb)


def _gmlp_p_kernel(u_ref, v_ref, ws_ref, bst_ref, lg_ref, lb_ref, o_ref):
    v = _ln(v_ref[...], lg_ref[...], lb_ref[...]).astype(BF16)
    tgt = lax.broadcasted_iota(I32, (CHUNK, CHUNK), 0)
    src = lax.broadcasted_iota(I32, (CHUNK, CHUNK), 1)
    causal = src <= tgt
    hd = DC // HEADS
    for h in range(HEADS):
        cs = slice(h * hd, (h + 1) * hd)
        ws = jnp.where(causal, ws_ref[h], 0.0).astype(BF16)
        mixed = jnp.dot(ws, v[:, cs], preferred_element_type=F32) + bst_ref[:, h:h + 1]
        o_ref[:, cs] = (u_ref[:, cs] * mixed).astype(o_ref.dtype)


def _gmlp_p(zb, ws, bst, lg, lb):
    return pl.pallas_call(
        _gmlp_p_kernel,
        grid=(RP // CHUNK,),
        in_specs=[
            pl.BlockSpec((CHUNK, DC), lambda i: (i, 0)),
            pl.BlockSpec((CHUNK, DC), lambda i: (i, 1)),
            _const_spec((HEADS, CHUNK, CHUNK)), _const_spec((CHUNK, HEADS)),
            _const_spec((1, DC)), _const_spec((1, DC)),
        ],
        out_specs=pl.BlockSpec((CHUNK, DC), lambda i: (i, 0)),
        out_shape=jax.ShapeDtypeStruct((RP, DC), BF16),
        compiler_params=_cp("arbitrary"),
        name="gmlp_prompt",
    )(zb, zb, ws, bst, lg, lb)


def _gmlp_s_kernel(u_ref, v_ref, coef_ref, bias_ref, lg_ref, lb_ref, o_ref, vo_ref):
    for l in range(LS):
        rows = slice(l * SUB, (l + 1) * SUB)
        vo_ref[rows, :] = _ln(v_ref[rows, :], lg_ref[...], lb_ref[...])
    for l in range(LS):
        rows = slice(l * SUB, (l + 1) * SUB)
        mixed = bias_ref[l:l + 1, :]
        for s in range(l + 1):
            mixed = mixed + coef_ref[l * LS + s:l * LS + s + 1, :] * vo_ref[s * SUB:(s + 1) * SUB, :]
        o_ref[rows, :] = (u_ref[rows, :] * mixed).astype(o_ref.dtype)


def _gmlp_s(zb, coef, bias, lg, lb):
    return pl.pallas_call(
        _gmlp_s_kernel,
        grid=(1,),
        in_specs=[
            pl.BlockSpec((RS, DC), lambda i: (RP // RS, 0)),
            pl.BlockSpec((RS, DC), lambda i: (RP // RS, 1)),
            _const_spec((LS * LS, DC)), _const_spec((LS, DC)), _const_spec((1, DC)), _const_spec((1, DC)),
        ],
        out_specs=[_const_spec((RS, DC)), _const_spec((RS, DC))],
        out_shape=[jax.ShapeDtypeStruct((RS, DC), BF16), jax.ShapeDtypeStruct((RS, DC), F32)],
        compiler_params=_cp("arbitrary"),
        name="gmlp_sample",
    )(zb, zb, coef, bias, lg, lb)


POOL_HALO = 16
GD = DC // len(POOL_WINDOWS)


def _pool_p_kernel(main_ref, halo_ref, o_ref, xc_ref):
    t_tile = pl.program_id(0) % (LP // SUB)
    xc_ref[0:POOL_HALO, :] = jnp.where(t_tile == 0, 0.0, halo_ref[...])
    xc_ref[POOL_HALO:, :] = main_ref[...]
    pos = t_tile * SUB + lax.broadcasted_iota(I32, (SUB, 1), 0)
    for g, w in enumerate(POOL_WINDOWS):
        cs = slice(g * GD, (g + 1) * GD)
        tok = xc_ref[POOL_HALO:POOL_HALO + SUB, cs]
        acc = tok
        for j in range(1, w):
            acc = acc + xc_ref[POOL_HALO - j:POOL_HALO - j + SUB, cs]
        cnt = jnp.minimum(pos + 1, w).astype(F32)
        o_ref[:, cs] = (acc / cnt - tok).astype(o_ref.dtype)


def _pool_p(p):
    halo_per_tile = SUB // POOL_HALO
    return pl.pallas_call(
        _pool_p_kernel,
        grid=(RP // SUB,),
        in_specs=[
            pl.BlockSpec((SUB, DC), lambda i: (i, 0)),
            pl.BlockSpec((POOL_HALO, DC), lambda i: (jnp.maximum(i * halo_per_tile - 1, 0), 0)),
        ],
        out_specs=pl.BlockSpec((SUB, DC), lambda i: (i, 0)),
        out_shape=jax.ShapeDtypeStruct((RP, DC), BF16),
        scratch_shapes=[pltpu.VMEM((POOL_HALO + SUB, DC), F32)],
        compiler_params=_cp("arbitrary"),
        name="pool_prompt",
    )(p, p)


def _pool_s_kernel(p_ref, st_ref, o_ref):
    def slab(j, cs):
        if j < HIST_P:
            return st_ref[j, :, cs]
        return p_ref[(j - HIST_P) * SUB:(j - HIST_P + 1) * SUB, cs]

    for l in range(LS):
        for g, w in enumerate(POOL_WINDOWS):
            cs = slice(g * GD, (g + 1) * GD)
            tok = slab(HIST_P + l, cs)
            acc = tok
            for j in range(1, w):
                acc = acc + slab(HIST_P + l - j, cs)
            o_ref[l * SUB:(l + 1) * SUB, cs] = (acc / float(w) - tok).astype(o_ref.dtype)


def _pool_s(p, state_t):
    return pl.pallas_call(
        _pool_s_kernel,
        grid=(1,),
        in_specs=[pl.BlockSpec((RS, DC), lambda i: (RP // RS, 0)), _const_spec((HIST_P, NS, DC))],
        out_specs=_const_spec((RS, DC)),
        out_shape=jax.ShapeDtypeStruct((RS, DC), BF16),
        compiler_params=_cp("arbitrary"),
        name="pool_sample",
    )(p, state_t)


def _merge_kernel(ap_ref, as_ref, bp_ref, bs_ref, cp_ref, cs_ref, ga_ref, gb_ref, gc_ref, x_ref,
                  gp_ref, gs_ref, gpost_ref, wpw_ref, wo_ref, wpool_ref, pscale_ref, wout_ref,
                  o_ref, m_ref):
    is_p = pl.program_id(0) < RP // TMM
    a = jnp.where(is_p, ap_ref[...], as_ref[...])
    b = jnp.where(is_p, bp_ref[...], bs_ref[...])
    c = jnp.where(is_p, cp_ref[...], cs_ref[...])
    m_ref[...] = ga_ref[...].astype(F32) * jnp.dot(a, wpw_ref[...], preferred_element_type=F32)
    m_ref[...] += gb_ref[...].astype(F32) * jnp.dot(b, wo_ref[...], preferred_element_type=F32)
    eo = D // len(POOL_WINDOWS)
    for g in range(len(POOL_WINDOWS)):
        os_ = slice(g * eo, (g + 1) * eo)
        yc = jnp.dot(c[:, g * GD:(g + 1) * GD], wpool_ref[g], preferred_element_type=F32)
        m_ref[:, os_] += gc_ref[:, os_].astype(F32) * (yc * pscale_ref[:, os_])
    mix = jnp.dot(m_ref[...].astype(BF16), wout_ref[...], preferred_element_type=F32)
    m_ref[...] = _rms(mix) * gpost_ref[...]
    gate = jnp.where(is_p, gp_ref[...], gs_ref[...])
    for s in range(TMM // SUB):
        rows = slice(s * SUB, (s + 1) * SUB)
        o_ref[rows, :] = x_ref[rows, :] + gate * m_ref[rows, :]


def _merge(acts, gates, x, ada_p, ada_s, gpost, wpw, wo, wpool, pscale, wout):
    (a_p, a_s), (b_p, b_s), (c_p, c_s) = acts
    ntp = RP // TMM
    pspec = pl.BlockSpec((TMM, DC), lambda i: (jnp.minimum(i, ntp - 1), 0))
    sspec = pl.BlockSpec((TMM, DC), lambda i: (jnp.maximum(i - ntp, 0), 0))
    gp, gs = _ada_specs(2, TMM)
    return pl.pallas_call(
        _merge_kernel,
        grid=(R // TMM,),
        in_specs=[
            pspec, sspec, pspec, sspec, pspec, sspec,
            pl.BlockSpec((TMM, D), lambda i: (i, 0)),
            pl.BlockSpec((TMM, D), lambda i: (i, 1)),
            pl.BlockSpec((TMM, D), lambda i: (i, 2)),
            pl.BlockSpec((TMM, D), lambda i: (i, 0)),
            gp, gs, _const_spec((1, D)),
            _const_spec((DC, D)), _const_spec((DC, D)), _const_spec((len(POOL_WINDOWS), GD, D // 4)),
            _const_spec((1, D)), _const_spec((D, D)),
        ],
        out_specs=pl.BlockSpec((TMM, D), lambda i: (i, 0)),
        out_shape=jax.ShapeDtypeStruct((R, D), F32),
        scratch_shapes=[pltpu.VMEM((TMM, D), F32)],
        compiler_params=_cp("arbitrary"),
        name="merge",
    )(a_p, a_s, b_p, b_s, c_p, c_s, gates, gates, gates, x, ada_p, ada_s, gpost,
      wpw, wo, wpool, pscale, wout)


def _route_kernel(x_ref, g_ref, scp_ref, scs_ref, shp_ref, shs_ref, wrh_ref, wrl_ref, rb_ref,
                  h_ref, hr_ref, e_ref, pos_ref, w_ref, cnt_ref, carry_ref):
    i = pl.program_id(0)

    @pl.when(i == 0)
    def _():
        carry_ref[...] = jnp.zeros_like(carry_ref)

    is_p = i < NTP
    sc = 1.0 + jnp.where(is_p, scp_ref[...], scs_ref[...])
    sh = jnp.where(is_p, shp_ref[...], shs_ref[...])
    g = g_ref[...]
    for s in range(NSUB):
        rows = slice(s * SUB, (s + 1) * SUB)
        h_ref[rows, :] = _rms(x_ref[rows, :]) * g * sc + sh
    for c in range(ROWC):
        hr_ref[pl.ds(c, TM, stride=ROWC), :] = h_ref[:, c * LANE:(c + 1) * LANE]

    h = h_ref[...]
    h_hi = h.astype(BF16)
    h_lo = (h - h_hi.astype(F32)).astype(BF16)
    nt = (((1,), (1,)), ((), ()))
    logits = (lax.dot_general(wrh_ref[...], h_hi, nt, preferred_element_type=F32)
              + lax.dot_general(wrl_ref[...], h_hi, nt, preferred_element_type=F32)
              + lax.dot_general(wrh_ref[...], h_lo, nt, preferred_element_type=F32))
    scores = jax.nn.sigmoid(logits)
    biased = scores + rb_ref[...]

    ge = NE // NGRP
    g3 = biased.reshape(NGRP, ge, TM)
    idx3 = lax.broadcasted_iota(I32, (NGRP, ge, TM), 1)
    m1 = jnp.max(g3, axis=1, keepdims=True)
    first = jnp.min(jnp.where(g3 == m1, idx3, ge), axis=1, keepdims=True)
    m2 = jnp.max(jnp.where(idx3 == first, -jnp.inf, g3), axis=1, keepdims=True)
    gscore = (m1 + m2).reshape(NGRP, TM)

    gidx = lax.broadcasted_iota(I32, (NGRP, TM), 0)
    grank = jnp.zeros((NGRP, TM), F32)
    for j in range(NGRP):
        sj = gscore[j:j + 1, :]
        grank = grank + jnp.where((sj > gscore) | ((sj == gscore) & (j < gidx)), 1.0, 0.0)
    gkeep = grank < TOPG
    ekeep = jnp.broadcast_to(gkeep.reshape(NGRP, 1, TM), (NGRP, ge, TM)).reshape(NE, TM)
    masked = jnp.where(ekeep, biased, -jnp.inf)

    eidx = lax.broadcasted_iota(I32, (NE, TM), 0)
    erank = jnp.zeros((NE, TM), F32)
    for j in range(NE):
        sj = masked[j:j + 1, :]
        erank = erank + jnp.where((sj > masked) | ((sj == masked) & (j < eidx)), 1.0, 0.0)
    sel = erank < TOPK

    sw = jnp.where(sel, scores, 0.0)
    cw = sw / jnp.sum(sw, axis=0, keepdims=True) * ROUTE_SCALE

    self_ = jnp.where(sel, 1.0, 0.0)
    t_src = lax.broadcasted_iota(I32, (TM, TM), 0)
    t_dst = lax.broadcasted_iota(I32, (TM, TM), 1)
    before = jnp.where(t_src < t_dst, 1.0, 0.0).astype(BF16)
    pos = jnp.dot(self_.astype(BF16), before, preferred_element_type=F32) + carry_ref[...]
    carry_ref[...] += jnp.sum(self_, axis=1, keepdims=True)
    cnt_ref[...] = jnp.broadcast_to(carry_ref[...], cnt_ref.shape).astype(I32)

    rem = sel
    for k in range(TOPK):
        ek = jnp.min(jnp.where(rem, eidx, NE), axis=0, keepdims=True)
        hit = eidx == ek
        e_ref[k:k + 1, :] = ek
        pos_ref[k:k + 1, :] = jnp.sum(jnp.where(hit, pos, 0.0), axis=0, keepdims=True).astype(I32)
        w_ref[k:k + 1, :] = jnp.sum(jnp.where(hit, cw, 0.0), axis=0, keepdims=True)
        rem = rem & jnp.logical_not(hit)


def _route(x, g, ada_p, ada_s, wr_hi, wr_lo, rbias):
    scp, scs = _ada_specs(4, TM)
    shp, shs = _ada_specs(3, TM)
    tok_spec = pl.BlockSpec((TOPK, TM), lambda i: (0, i))
    return pl.pallas_call(
        _route_kernel,
        grid=(NT,),
        in_specs=[
            pl.BlockSpec((TM, D), lambda i: (i, 0)), _const_spec((1, D)), scp, scs, shp, shs,
            _const_spec((NE, D)), _const_spec((NE, D)), _const_spec((NE, 1)),
        ],
        out_specs=[pl.BlockSpec((TM, D), lambda i: (i, 0)),
                   pl.BlockSpec((TM * ROWC, LANE), lambda i: (i, 0)),
                   tok_spec, tok_spec, tok_spec, _const_spec((NE, LANE))],
        out_shape=[
            jax.ShapeDtypeStruct((R, D), F32),
            jax.ShapeDtypeStruct((R * ROWC, LANE), F32),
            jax.ShapeDtypeStruct((TOPK, R), I32),
            jax.ShapeDtypeStruct((TOPK, R), I32),
            jax.ShapeDtypeStruct((TOPK, R), F32),
            jax.ShapeDtypeStruct((NE, LANE), I32),
        ],
        scratch_shapes=[pltpu.VMEM((NE, 1), F32)],
        compiler_params=_cp("arbitrary"),
        name="route",
    )(x, g, ada_p, ada_s, ada_p, ada_s, wr_hi, wr_lo, rbias)


def _rows(ref, first_row, n_rows):
    return ref.at[pl.ds(pl.multiple_of(first_row * ROWC, ROWC), n_rows * ROWC), :]


def _wait_rows(src_like, dst_like, sem, n_tiles):
    for _ in range(n_tiles):
        pltpu.make_async_copy(src_like, dst_like, sem).wait()


def _dispatch_kernel(pstart_ref, cnt_ref, h_ref, e_ref, pos_ref, xb_ref, z_ref, sem, zsem):
    @pl.when(pl.program_id(0) == 0)
    def _():
        z_ref[...] = jnp.zeros_like(z_ref)

        def per_expert(e, carry):
            n = cnt_ref[e]
            npad = (BLK - n % BLK) % BLK
            off = pstart_ref[e] + n
            bit = BLK // 2
            while bit >= 1:
                @pl.when((npad & bit) != 0)
                def _(off=off, bit=bit):
                    cp = pltpu.make_async_copy(_rows(z_ref, 0, bit), _rows(xb_ref, off, bit), zsem)
                    cp.start()
                    cp.wait()
                off = off + (npad & bit)
                bit //= 2
            return carry

        lax.fori_loop(0, NE, per_expert, 0)

    def per_token(r, carry):
        for k in range(TOPK):
            dest = pstart_ref[e_ref[k, r]] + pos_ref[k, r]
            pltpu.make_async_copy(_rows(h_ref, r, 1), _rows(xb_ref, dest, 1), sem).start()
        return carry

    lax.fori_loop(0, TK, per_token, 0)
    _wait_rows(h_ref, _rows(xb_ref, 0, TK), sem, TOPK)


def _dispatch(pstart, counts, h2r, e3, pos3):
    smem_spec = pl.BlockSpec((None, TOPK, TK), lambda i, *_: (i, 0, 0), memory_space=pltpu.SMEM)
    return pl.pallas_call(
        _dispatch_kernel,
        grid_spec=pltpu.PrefetchScalarGridSpec(
            num_scalar_prefetch=2,
            grid=(NTK,),
            in_specs=[pl.BlockSpec((TK * ROWC, LANE), lambda i, *_: (i, 0)), smem_spec, smem_spec],
            out_specs=pl.BlockSpec(memory_space=pl.ANY),
            scratch_shapes=[pltpu.VMEM((BLK // 2 * ROWC, LANE), F32),
                            pltpu.SemaphoreType.DMA, pltpu.SemaphoreType.DMA],
        ),
        out_shape=jax.ShapeDtypeStruct((NROWS * ROWC, LANE), F32),
        compiler_params=_cp("arbitrary"),
        name="dispatch",
    )(pstart, counts, h2r, e3, pos3)


def _experts_kernel(be_ref, nu_ref, xb_ref, w1_ref, w3_ref, w2_ref, yb_ref, w1b_ref, w3b_ref, w2b_ref,
                    xs_ref):
    b = pl.program_id(0)
    live = b < nu_ref[0]
    fresh = (b == 0) | (be_ref[b] != be_ref[jnp.maximum(b - 1, 0)])

    @pl.when(live & fresh)
    def _():
        w1b_ref[...] = w1_ref[...].astype(BF16)
        w3b_ref[...] = w3_ref[...].astype(BF16)
        w2b_ref[...] = w2_ref[...].astype(BF16)

    @pl.when(live)
    def _():
        for c in range(ROWC):
            xs_ref[:, c * LANE:(c + 1) * LANE] = xb_ref[pl.ds(c, BLK, stride=ROWC), :].astype(BF16)
        x = xs_ref[...]
        hid = (_silu(jnp.dot(x, w1b_ref[...], preferred_element_type=F32))
               * jnp.dot(x, w3b_ref[...], preferred_element_type=F32)).astype(BF16)
        for c2 in range(D // MXU_N):
            y = jnp.dot(hid, w2b_ref[:, c2 * MXU_N:(c2 + 1) * MXU_N], preferred_element_type=F32)
            for q in range(MXU_N // LANE):
                c = c2 * (MXU_N // LANE) + q
                yb_ref[pl.ds(c, BLK, stride=ROWC), :] = y[:, q * LANE:(q + 1) * LANE]


def _experts(blk_e, n_used, xb, w1, w3, w2, layer):
    row_map = lambda b, be, nu: (jnp.minimum(b, nu[0] - 1), 0)
    w_map = lambda b, be, nu: (layer, be[b], 0, 0)
    return pl.pallas_call(
        _experts_kernel,
        grid_spec=pltpu.PrefetchScalarGridSpec(
            num_scalar_prefetch=2,
            grid=(NB,),
            in_specs=[
                pl.BlockSpec((BLK * ROWC, LANE), row_map),
                pl.BlockSpec((None, None, D, DE), w_map),
                pl.BlockSpec((None, None, D, DE), w_map),
                pl.BlockSpec((None, None, DE, D), w_map),
            ],
            out_specs=pl.BlockSpec((BLK * ROWC, LANE), row_map),
            scratch_shapes=[pltpu.VMEM((D, DE), BF16), pltpu.VMEM((D, DE), BF16), pltpu.VMEM((DE, D), BF16),
                            pltpu.VMEM((BLK, D), BF16)],
        ),
        out_shape=jax.ShapeDtypeStruct((NROWS * ROWC, LANE), F32),
        compiler_params=_cp("arbitrary"),
        name="experts",
    )(blk_e, n_used, xb, w1, w3, w2)


def _shared_kernel(h_ref, w1_ref, w3_ref, w2_ref, o_ref):
    x = h_ref[...].astype(BF16)
    hid = (_silu(jnp.dot(x, w1_ref[...], preferred_element_type=F32))
           * jnp.dot(x, w3_ref[...], preferred_element_type=F32))
    o_ref[...] = jnp.dot(hid.astype(BF16), w2_ref[...], preferred_element_type=F32)


def _shared(h2, ws1, ws3, ws2):
    return pl.pallas_call(
        _shared_kernel,
        grid=(NT,),
        in_specs=[pl.BlockSpec((TM, D), lambda i: (i, 0)),
                  _const_spec((D, DE)), _const_spec((D, DE)), _const_spec((DE, D))],
        out_specs=pl.BlockSpec((TM, D), lambda i: (i, 0)),
        out_shape=jax.ShapeDtypeStruct((R, D), F32),
        compiler_params=_cp("arbitrary"),
        name="shared",
    )(h2, ws1, ws3, ws2)


def _combine_kernel(pstart_ref, x_ref, sh_ref, wt_ref, e_ref, pos_ref, gp_ref, gs_ref, gpost_ref, yb_ref,
                    o_ref, buf_ref, f_ref, sem):
    def per_token(r, carry):
        for k in range(TOPK):
            src = pstart_ref[e_ref[k, r]] + pos_ref[k, r]
            pltpu.make_async_copy(_rows(yb_ref, src, 1), _rows(buf_ref.at[k], r, 1), sem).start()
        return carry

    lax.fori_loop(0, TK, per_token, 0)
    _wait_rows(_rows(yb_ref, 0, TK), buf_ref.at[0], sem, TOPK)

    for c in range(ROWC):
        cs = slice(c * LANE, (c + 1) * LANE)
        acc = sh_ref[:, cs]
        for k in range(TOPK):
            acc = acc + wt_ref[:, k:k + 1] * buf_ref[k, pl.ds(c, TK, stride=ROWC), :]
        f_ref[:, cs] = acc
    gate = jnp.where(pl.program_id(0) < NTKP, gp_ref[...], gs_ref[...])
    o_ref[...] = x_ref[...] + gate * (_rms(f_ref[...]) * gpost_ref[...])


def _combine(pstart, x, shared, wt, e3, pos3, ada_p, ada_s, gpost, yb):
    smem_spec = pl.BlockSpec((None, TOPK, TK), lambda i, *_: (i, 0, 0), memory_space=pltpu.SMEM)
    tiles_per_seq = LP // TK
    gp = pl.BlockSpec((None, 1, D), lambda i, *_: (jnp.minimum(i // tiles_per_seq, NP - 1), 0, 5))
    gs = pl.BlockSpec((NS, D), lambda i, *_: (0, 5))
    row = pl.BlockSpec((TK, D), lambda i, *_: (i, 0))
    return pl.pallas_call(
        _combine_kernel,
        grid_spec=pltpu.PrefetchScalarGridSpec(
            num_scalar_prefetch=1,
            grid=(NTK,),
            in_specs=[row, row, pl.BlockSpec((TK, TOPK), lambda i, *_: (i, 0)), smem_spec, smem_spec,
                      gp, gs, pl.BlockSpec((1, D), lambda i, *_: (0, 0)),
                      pl.BlockSpec(memory_space=pl.ANY)],
            out_specs=row,
            scratch_shapes=[pltpu.VMEM((TOPK, TK * ROWC, LANE), F32), pltpu.VMEM((TK, D), F32),
                            pltpu.SemaphoreType.DMA],
        ),
        out_shape=jax.ShapeDtypeStruct((R, D), F32),
        compiler_params=_cp("arbitrary"),
        name="combine",
    )(pstart, x, shared, wt, e3, pos3, ada_p, ada_s, gpost, yb)


def _row(v):
    return v.reshape(1, -1)


def _to_rows(x_prompt, x_sample):
    return jnp.concatenate([x_prompt.reshape(RP, D), x_sample.transpose(1, 0, 2).reshape(RS, D)], axis=0)


def _sample_rows_to_batch(rows, width):
    return rows.reshape(LS, NS, width).transpose(1, 0, 2)


def kernel(x_prompt, x_sample, state_conv, state_pool, c_prompt, c_sample, w_ada, b_ada, g_pre_mix,
           g_post_mix, g_pre_ffn, g_post_ffn, w_in, conv_w, conv_b, conv_ln_g, conv_ln_b, conv_w_pw,
           gmlp_ln_g, gmlp_ln_b, gmlp_w_s, gmlp_b_s, gmlp_w_o, pool_w, pool_scale, w_out, router_w,
           router_bias, w1, w3, w2, ws1, ws3, ws2):
    x = _to_rows(x_prompt, x_sample)
    ada = _ada(jnp.concatenate([c_prompt, c_sample], axis=0), w_ada, b_ada)

    conv_p, pool_p, conv_s, pool_s, v_s = [], [], [], [], []
    for l in range(DEPTH):
        ada_p = ada[l, :NP].reshape(NP, 1, 6 * D)
        ada_s = ada[l, NP:]

        h = _hmod(x, _row(g_pre_mix[l]), ada_p, ada_s, 1, 0)
        glu = _colmm(h, w_in, l, (0, DC), DC, 512, _glu_epilogue, F32, "in_glu")
        zb = _colmm(h, w_in, l, (COL_GMLP,), 2 * DC, 1024, _gelu_epilogue, F32, "in_gelu")
        p = _colmm(h, w_in, l, (COL_POOL,), DC, 1024, _id_epilogue, F32, "in_pool")
        gates = _colmm(h, w_in, l, (COL_GATE,), 3 * D, 1024, _sigmoid_epilogue, BF16, "in_gate")

        cw, cb = conv_w[l], _row(conv_b[l])
        clg, clb = _row(conv_ln_g[l]), _row(conv_ln_b[l])
        a_p = _conv_p(glu, cw, cb, clg, clb)
        a_s = _conv_s(glu, state_conv[l].transpose(1, 0, 2), cw, cb, clg, clb)

        glg, glb = _row(gmlp_ln_g[l]), _row(gmlp_ln_b[l])
        b_p = _gmlp_p(zb, gmlp_w_s[l], gmlp_b_s[l].T, glg, glb)
        coef = jnp.repeat(gmlp_w_s[l][:, :LS, :LS].transpose(1, 2, 0).reshape(LS * LS, HEADS), DC // HEADS, axis=1)
        bias = jnp.repeat(gmlp_b_s[l][:, :LS].T, DC // HEADS, axis=1)
        b_s, v_rows = _gmlp_s(zb, coef, bias, glg, glb)

        c_p = _pool_p(p)
        c_s = _pool_s(p, state_pool[l].transpose(1, 0, 2))

        x = _merge(((a_p, a_s), (b_p, b_s), (c_p, c_s)), gates, x, ada_p, ada_s, _row(g_post_mix[l]),
                   conv_w_pw[l].astype(BF16), gmlp_w_o[l].astype(BF16), pool_w[l].astype(BF16),
                   _row(pool_scale[l]), w_out[l].astype(BF16))

        glu_p = glu[:RP].reshape(NP, LP, DC)
        p_p = p[:RP].reshape(NP, LP, DC)
        conv_p.append(glu_p[:, LP - HIST_C:])
        pool_p.append(p_p[:, LP - HIST_P:])
        conv_s.append(jnp.concatenate([state_conv[l][:, LS:], _sample_rows_to_batch(glu[RP:], DC)], axis=1))
        pool_s.append(jnp.concatenate([state_pool[l][:, LS:], _sample_rows_to_batch(p[RP:], DC)], axis=1))
        v_s.append(_sample_rows_to_batch(v_rows, DC))

        wr_t = router_w[l].T
        wr_hi = wr_t.astype(BF16)
        wr_lo = (wr_t - wr_hi.astype(F32)).astype(BF16)
        h2, h2r, e_t, pos_t, w_t, cnt = _route(x, _row(g_pre_ffn[l]), ada_p, ada_s, wr_hi, wr_lo,
                                          router_bias[l].reshape(NE, 1))
        counts = cnt[:, 0]
        padded = (counts + BLK - 1) // BLK * BLK
        pad_end = jnp.cumsum(padded)
        pstart = (pad_end - padded).astype(I32)
        n_used = (pad_end[-1] // BLK).astype(I32)
        blk = jnp.minimum(jnp.arange(NB, dtype=I32), n_used - 1)
        blk_e = jnp.minimum(jnp.searchsorted(pad_end, blk * BLK, side='right'), NE - 1).astype(I32)
        e3 = e_t.reshape(TOPK, NTK, TK).transpose(1, 0, 2)
        pos3 = pos_t.reshape(TOPK, NTK, TK).transpose(1, 0, 2)

        xb = _dispatch(pstart, counts, h2r, e3, pos3)
        yb = _experts(blk_e, n_used.reshape(1), xb, w1, w3, w2, l)
        shared = _shared(h2, ws1[l].astype(BF16), ws3[l].astype(BF16), ws2[l].astype(BF16))
        x = _combine(pstart, x, shared, w_t.T, e3, pos3, ada_p, ada_s, _row(g_post_ffn[l]), yb)

    y_prompt = x[:RP].reshape(NP, LP, D)
    y_sample = _sample_rows_to_batch(x[RP:], D)
    return (y_prompt, y_sample, jnp.stack(conv_p), jnp.stack(pool_p), jnp.stack(conv_s),
            jnp.stack(pool_s), jnp.stack(v_s))
```

```python
import functools

import jax
import jax.numpy as jnp
from jax import lax
from jax.experimental import pallas as pl
from jax.experimental.pallas import tpu as pltpu

F32 = jnp.float32
BF16 = jnp.bfloat16
I32 = jnp.int32

D = 2048
NP, LP = 4, 2048
NS, LS = 128, 4
RP = NP * LP
RS = NS * LS
R = RP + RS
DEPTH = 2
DC = 1024
CONV_W = 31
HIST_C = CONV_W - 1
HIST_P = 15
POOL_WINDOWS = (2, 4, 8, 16)
HEADS = 8
CHUNK = 128
D_IN = 5 * DC + 3 * D
COL_GMLP, COL_POOL, COL_GATE = 2 * DC, 4 * DC, 5 * DC
NE, TOPK, NGRP, TOPG = 64, 8, 8, 4
DE = 512
ROUTE_SCALE = 2.5
EPS = 1e-6

TM = 512
NT = R // TM
NTP = RP // TM
SUB = 128
NSUB = TM // SUB
TMM = 256
LANE = 128
SUBLANE = 8
MXU_N = 256
TS = 256
NTS = R // TS
SEG = TOPK * TS + NE * SUBLANE
DH = D // 2
BLK = 256
NB = -(-(R * TOPK + NTS * NE * (SUBLANE - 1) + NE * (BLK - 1)) // BLK)
NROWS = NB * BLK

VMEM_LIMIT = 56 * 1024 * 1024


def _cp(*sem):
    return pltpu.CompilerParams(dimension_semantics=sem, vmem_limit_bytes=VMEM_LIMIT)


def _const_spec(shape):
    nd = len(shape)
    return pl.BlockSpec(shape, lambda *_: (0,) * nd, pipeline_mode=pl.Buffered(1))


def _rms(x):
    return x * lax.rsqrt(jnp.mean(x * x, axis=-1, keepdims=True) + EPS)


def _ln(x, g, b):
    mu = jnp.mean(x, axis=-1, keepdims=True)
    xc = x - mu
    var = jnp.mean(xc * xc, axis=-1, keepdims=True)
    return xc * lax.rsqrt(var + EPS) * g + b


def _silu(x):
    return x * jax.nn.sigmoid(x)


ADA_TN = 1024


def _ada_kernel(c_ref, w_ref, b_ref, o_ref):
    c = _silu(c_ref[...]).astype(BF16)
    o_ref[...] = jnp.dot(c, w_ref[...].astype(BF16), preferred_element_type=F32) + b_ref[...]


def _ada(c_all, w_ada, b_ada):
    n = c_all.shape[0]
    return pl.pallas_call(
        _ada_kernel,
        grid=(DEPTH, 6 * D // ADA_TN),
        in_specs=[
            pl.BlockSpec((n, D), lambda l, j: (0, 0)),
            pl.BlockSpec((None, D, ADA_TN), lambda l, j: (l, 0, j)),
            pl.BlockSpec((None, 1, ADA_TN), lambda l, j: (l, 0, j)),
        ],
        out_specs=pl.BlockSpec((None, n, ADA_TN), lambda l, j: (l, 0, j)),
        out_shape=jax.ShapeDtypeStruct((DEPTH, n, 6 * D), F32),
        compiler_params=_cp("arbitrary", "arbitrary"),
        name="ada",
    )(c_all, w_ada, b_ada.reshape(DEPTH, 1, 6 * D))


def _ada_specs(chunk, tile_rows):
    tiles_per_seq = LP // tile_rows
    p = pl.BlockSpec((None, 1, D), lambda i: (jnp.minimum(i // tiles_per_seq, NP - 1), 0, chunk))
    s = pl.BlockSpec((NS, D), lambda i: (0, chunk))
    return p, s


def _hmod_kernel(x_ref, g_ref, scp_ref, scs_ref, shp_ref, shs_ref, o_ref):
    is_p = pl.program_id(0) < NTP
    sc = 1.0 + jnp.where(is_p, scp_ref[...], scs_ref[...])
    sh = jnp.where(is_p, shp_ref[...], shs_ref[...])
    g = g_ref[...]
    for s in range(NSUB):
        rows = slice(s * SUB, (s + 1) * SUB)
        o_ref[rows, :] = (_rms(x_ref[rows, :]) * g * sc + sh).astype(o_ref.dtype)


def _hmod(x, g, ada_p, ada_s, c_scale, c_shift):
    scp, scs = _ada_specs(c_scale, TM)
    shp, shs = _ada_specs(c_shift, TM)
    return pl.pallas_call(
        _hmod_kernel,
        grid=(NT,),
        in_specs=[pl.BlockSpec((TM, D), lambda i: (i, 0)), _const_spec((1, D)), scp, scs, shp, shs],
        out_specs=pl.BlockSpec((TM, D), lambda i: (i, 0)),
        out_shape=jax.ShapeDtypeStruct((R, D), BF16),
        compiler_params=_cp("arbitrary"),
        name="hmod",
    )(x, g, ada_p, ada_s, ada_p, ada_s)


def _colmm_kernel(nw, epilogue, *refs):
    lhs_ref = refs[0]
    w_refs = refs[1:1 + nw]
    o_ref = refs[1 + nw]
    wb_refs = refs[2 + nw:]

    @pl.when(pl.program_id(1) == 0)
    def _():
        for w_ref, wb_ref in zip(w_refs, wb_refs):
            wb_ref[...] = w_ref[...].astype(BF16)

    lhs = lhs_ref[...]
    accs = [jnp.dot(lhs, wb_ref[...], preferred_element_type=F32) for wb_ref in wb_refs]
    o_ref[...] = epilogue(*accs).astype(o_ref.dtype)


def _colmm(h, w_in, layer, col_offsets, n_cols, tn, epilogue, out_dtype, name):
    nw = len(col_offsets)
    w_specs = [
        pl.BlockSpec((None, D, tn), functools.partial(lambda j, i, o: (layer, 0, o + j), o=off // tn))
        for off in col_offsets
    ]
    return pl.pallas_call(
        functools.partial(_colmm_kernel, nw, epilogue),
        grid=(n_cols // tn, NT),
        in_specs=[pl.BlockSpec((TM, D), lambda j, i: (i, 0))] + w_specs,
        out_specs=pl.BlockSpec((TM, tn), lambda j, i: (i, j)),
        out_shape=jax.ShapeDtypeStruct((R, n_cols), out_dtype),
        scratch_shapes=[pltpu.VMEM((D, tn), BF16) for _ in range(nw)],
        compiler_params=_cp("arbitrary", "arbitrary"),
        name=name,
    )(h, *([w_in] * nw))


def _glu_epilogue(a, g):
    return a * jax.nn.sigmoid(g)


def _gelu_epilogue(z):
    return 0.5 * z * (1.0 + lax.erf(z * (0.5 ** 0.5)))


def _id_epilogue(z):
    return z


def _sigmoid_epilogue(z):
    return jax.nn.sigmoid(z)


CONV_HALO = 32


def _conv_taps(w_ref, slab, cs):
    acc = None
    for k in range(CONV_W):
        term = w_ref[k:k + 1, cs] * slab(k)
        acc = term if acc is None else acc + term
    return acc


def _conv_p_kernel(main_ref, halo_ref, w_ref, b_ref, lg_ref, lb_ref, o_ref, xp_ref, y_ref):
    first = (pl.program_id(0) % (LP // SUB)) == 0
    xp_ref[0:CONV_HALO, :] = jnp.where(first, 0.0, halo_ref[...])
    xp_ref[CONV_HALO:, :] = main_ref[...]
    base = CONV_HALO - HIST_C
    for c in range(DC // LANE):
        cs = slice(c * LANE, (c + 1) * LANE)
        acc = _conv_taps(w_ref, lambda k: xp_ref[base + k:base + k + SUB, cs], cs)
        y_ref[:, cs] = acc + b_ref[:, cs]
    o_ref[...] = _silu(_ln(y_ref[...], lg_ref[...], lb_ref[...])).astype(o_ref.dtype)


def _conv_p(glu, w, b, lg, lb):
    halo_per_tile = SUB // CONV_HALO
    return pl.pallas_call(
        _conv_p_kernel,
        grid=(RP // SUB,),
        in_specs=[
            pl.BlockSpec((SUB, DC), lambda i: (i, 0)),
            pl.BlockSpec((CONV_HALO, DC), lambda i: (jnp.maximum(i * halo_per_tile - 1, 0), 0)),
            _const_spec((CONV_W, DC)), _const_spec((1, DC)), _const_spec((1, DC)), _const_spec((1, DC)),
        ],
        out_specs=pl.BlockSpec((SUB, DC), lambda i: (i, 0)),
        out_shape=jax.ShapeDtypeStruct((RP, DC), BF16),
        scratch_shapes=[pltpu.VMEM((CONV_HALO + SUB, DC), F32), pltpu.VMEM((SUB, DC), F32)],
        compiler_params=_cp("arbitrary"),
        name="conv_prompt",
    )(glu, glu, w, b, lg, lb)


def _conv_s_kernel(glu_ref, st_ref, w_ref, b_ref, lg_ref, lb_ref, o_ref, y_ref):
    for l in range(LS):
        rows = slice(l * SUB, (l + 1) * SUB)

        def slab(k, cs):
            j = l + k
            if j < HIST_C:
                return st_ref[j, :, cs]
            return glu_ref[(j - HIST_C) * SUB:(j - HIST_C + 1) * SUB, cs]

        for c in range(DC // LANE):
            cs = slice(c * LANE, (c + 1) * LANE)
            acc = _conv_taps(w_ref, functools.partial(slab, cs=cs), cs)
            y_ref[:, cs] = acc + b_ref[:, cs]
        o_ref[rows, :] = _silu(_ln(y_ref[...], lg_ref[...], lb_ref[...])).astype(o_ref.dtype)


def _conv_s(glu, state_t, w, b, lg, lb):
    return pl.pallas_call(
        _conv_s_kernel,
        grid=(1,),
        in_specs=[
            pl.BlockSpec((RS, DC), lambda i: (RP // RS, 0)),
            _const_spec((HIST_C, NS, DC)),
            _const_spec((CONV_W, DC)), _const_spec((1, DC)), _const_spec((1, DC)), _const_spec((1, DC)),
        ],
        out_specs=_const_spec((RS, DC)),
        out_shape=jax.ShapeDtypeStruct((RS, DC), BF16),
        scratch_shapes=[pltpu.VMEM((SUB, DC), F32)],
        compiler_params=_cp("arbitrary"),
        name="conv_sample",
    )(glu, state_t, w, b, lg, lb)


def _gmlp_p_kernel(u_ref, v_ref, ws_ref, bst_ref, lg_ref, lb_ref, o_ref):
    v = _ln(v_ref[...], lg_ref[...], lb_ref[...]).astype(BF16)
    tgt = lax.broadcasted_iota(I32, (CHUNK, CHUNK), 0)
    src = lax.broadcasted_iota(I32, (CHUNK, CHUNK), 1)
    causal = src <= tgt
    hd = DC // HEADS
    for h in range(HEADS):
        cs = slice(h * hd, (h + 1) * hd)
        ws = jnp.where(causal, ws_ref[h], 0.0).astype(BF16)
        mixed = jnp.dot(ws, v[:, cs], preferred_element_type=F32) + bst_ref[:, h:h + 1]
        o_ref[:, cs] = (u_ref[:, cs] * mixed).astype(o_ref.dtype)


def _gmlp_p(zb, ws, bst, lg, lb):
    return pl.pallas_call(
        _gmlp_p_kernel,
        grid=(RP // CHUNK,),
        in_specs=[
            pl.BlockSpec((CHUNK, DC), lambda i: (i, 0)),
            pl.BlockSpec((CHUNK, DC), lambda i: (i, 1)),
            _const_spec((HEADS, CHUNK, CHUNK)), _const_spec((CHUNK, HEADS)),
            _const_spec((1, DC)), _const_spec((1, DC)),
        ],
        out_specs=pl.BlockSpec((CHUNK, DC), lambda i: (i, 0)),
        out_shape=jax.ShapeDtypeStruct((RP, DC), BF16),
        compiler_params=_cp("arbitrary"),
        name="gmlp_prompt",
    )(zb, zb, ws, bst, lg, lb)


def _gmlp_s_kernel(u_ref, v_ref, coef_ref, bias_ref, lg_ref, lb_ref, o_ref, vo_ref):
    for l in range(LS):
        rows = slice(l * SUB, (l + 1) * SUB)
        vo_ref[rows, :] = _ln(v_ref[rows, :], lg_ref[...], lb_ref[...])
    for l in range(LS):
        rows = slice(l * SUB, (l + 1) * SUB)
        mixed = bias_ref[l:l + 1, :]
        for s in range(l + 1):
            mixed = mixed + coef_ref[l * LS + s:l * LS + s + 1, :] * vo_ref[s * SUB:(s + 1) * SUB, :]
        o_ref[rows, :] = (u_ref[rows, :] * mixed).astype(o_ref.dtype)


def _gmlp_s(zb, coef, bias, lg, lb):
    return pl.pallas_call(
        _gmlp_s_kernel,
        grid=(1,),
        in_specs=[
            pl.BlockSpec((RS, DC), lambda i: (RP // RS, 0)),
            pl.BlockSpec((RS, DC), lambda i: (RP // RS, 1)),
            _const_spec((LS * LS, DC)), _const_spec((LS, DC)), _const_spec((1, DC)), _const_spec((1, DC)),
        ],
        out_specs=[_const_spec((RS, DC)), _const_spec((RS, DC))],
        out_shape=[jax.ShapeDtypeStruct((RS, DC), BF16), jax.ShapeDtypeStruct((RS, DC), F32)],
        compiler_params=_cp("arbitrary"),
        name="gmlp_sample",
    )(zb, zb, coef, bias, lg, lb)


POOL_HALO = 16
GD = DC // len(POOL_WINDOWS)


def _pool_p_kernel(main_ref, halo_ref, o_ref, xc_ref):
    t_tile = pl.program_id(0) % (LP // SUB)
    xc_ref[0:POOL_HALO, :] = jnp.where(t_tile == 0, 0.0, halo_ref[...])
    xc_ref[POOL_HALO:, :] = main_ref[...]
    pos = t_tile * SUB + lax.broadcasted_iota(I32, (SUB, 1), 0)
    for g, w in enumerate(POOL_WINDOWS):
        cs = slice(g * GD, (g + 1) * GD)
        tok = xc_ref[POOL_HALO:POOL_HALO + SUB, cs]
        acc = tok
        for j in range(1, w):
            acc = acc + xc_ref[POOL_HALO - j:POOL_HALO - j + SUB, cs]
        cnt = jnp.minimum(pos + 1, w).astype(F32)
        o_ref[:, cs] = (acc / cnt - tok).astype(o_ref.dtype)


def _pool_p(p):
    halo_per_tile = SUB // POOL_HALO
    return pl.pallas_call(
        _pool_p_kernel,
        grid=(RP // SUB,),
        in_specs=[
            pl.BlockSpec((SUB, DC), lambda i: (i, 0)),
            pl.BlockSpec((POOL_HALO, DC), lambda i: (jnp.maximum(i * halo_per_tile - 1, 0), 0)),
        ],
        out_specs=pl.BlockSpec((SUB, DC), lambda i: (i, 0)),
        out_shape=jax.ShapeDtypeStruct((RP, DC), BF16),
        scratch_shapes=[pltpu.VMEM((POOL_HALO + SUB, DC), F32)],
        compiler_params=_cp("arbitrary"),
        name="pool_prompt",
    )(p, p)


def _pool_s_kernel(p_ref, st_ref, o_ref):
    def slab(j, cs):
        if j < HIST_P:
            return st_ref[j, :, cs]
        return p_ref[(j - HIST_P) * SUB:(j - HIST_P + 1) * SUB, cs]

    for l in range(LS):
        for g, w in enumerate(POOL_WINDOWS):
            cs = slice(g * GD, (g + 1) * GD)
            tok = slab(HIST_P + l, cs)
            acc = tok
            for j in range(1, w):
                acc = acc + slab(HIST_P + l - j, cs)
            o_ref[l * SUB:(l + 1) * SUB, cs] = (acc / float(w) - tok).astype(o_ref.dtype)


def _pool_s(p, state_t):
    return pl.pallas_call(
        _pool_s_kernel,
        grid=(1,),
        in_specs=[pl.BlockSpec((RS, DC), lambda i: (RP // RS, 0)), _const_spec((HIST_P, NS, DC))],
        out_specs=_const_spec((RS, DC)),
        out_shape=jax.ShapeDtypeStruct((RS, DC), BF16),
        compiler_params=_cp("arbitrary"),
        name="pool_sample",
    )(p, state_t)


def _merge_kernel(ap_ref, as_ref, bp_ref, bs_ref, cp_ref, cs_ref, ga_ref, gb_ref, gc_ref, x_ref,
                  gp_ref, gs_ref, gpost_ref, wpw_ref, wo_ref, wpool_ref, pscale_ref, wout_ref,
                  o_ref, m_ref):
    is_p = pl.program_id(0) < RP // TMM
    a = jnp.where(is_p, ap_ref[...], as_ref[...])
    b = jnp.where(is_p, bp_ref[...], bs_ref[...])
    c = jnp.where(is_p, cp_ref[...], cs_ref[...])
    m_ref[...] = ga_ref[...].astype(F32) * jnp.dot(a, wpw_ref[...], preferred_element_type=F32)
    m_ref[...] += gb_ref[...].astype(F32) * jnp.dot(b, wo_ref[...], preferred_element_type=F32)
    eo = D // len(POOL_WINDOWS)
    for g in range(len(POOL_WINDOWS)):
        os_ = slice(g * eo, (g + 1) * eo)
        yc = jnp.dot(c[:, g * GD:(g + 1) * GD], wpool_ref[g], preferred_element_type=F32)
        m_ref[:, os_] += gc_ref[:, os_].astype(F32) * (yc * pscale_ref[:, os_])
    mix = jnp.dot(m_ref[...].astype(BF16), wout_ref[...], preferred_element_type=F32)
    m_ref[...] = _rms(mix) * gpost_ref[...]
    gate = jnp.where(is_p, gp_ref[...], gs_ref[...])
    for s in range(TMM // SUB):
        rows = slice(s * SUB, (s + 1) * SUB)
        o_ref[rows, :] = x_ref[rows, :] + gate * m_ref[rows, :]


def _merge(acts, gates, x, ada_p, ada_s, gpost, wpw, wo, wpool, pscale, wout):
    (a_p, a_s), (b_p, b_s), (c_p, c_s) = acts
    ntp = RP // TMM
    pspec = pl.BlockSpec((TMM, DC), lambda i: (jnp.minimum(i, ntp - 1), 0))
    sspec = pl.BlockSpec((TMM, DC), lambda i: (jnp.maximum(i - ntp, 0), 0))
    gp, gs = _ada_specs(2, TMM)
    return pl.pallas_call(
        _merge_kernel,
        grid=(R // TMM,),
        in_specs=[
            pspec, sspec, pspec, sspec, pspec, sspec,
            pl.BlockSpec((TMM, D), lambda i: (i, 0)),
            pl.BlockSpec((TMM, D), lambda i: (i, 1)),
            pl.BlockSpec((TMM, D), lambda i: (i, 2)),
            pl.BlockSpec((TMM, D), lambda i: (i, 0)),
            gp, gs, _const_spec((1, D)),
            _const_spec((DC, D)), _const_spec((DC, D)), _const_spec((len(POOL_WINDOWS), GD, D // 4)),
            _const_spec((1, D)), _const_spec((D, D)),
        ],
        out_specs=pl.BlockSpec((TMM, D), lambda i: (i, 0)),
        out_shape=jax.ShapeDtypeStruct((R, D), F32),
        scratch_shapes=[pltpu.VMEM((TMM, D), F32)],
        compiler_params=_cp("arbitrary"),
        name="merge",
    )(a_p, a_s, b_p, b_s, c_p, c_s, gates, gates, gates, x, ada_p, ada_s, gpost,
      wpw, wo, wpool, pscale, wout)


def _route_kernel(x_ref, g_ref, scp_ref, scs_ref, shp_ref, shs_ref, wrh_ref, wrl_ref, rb_ref,
                  hb_ref, slot_ref, w_ref, meta_ref, cnt_ref, h_ref, carry_ref):
    TM = TS
    i = pl.program_id(0)

    @pl.when(i == 0)
    def _():
        carry_ref[...] = jnp.zeros_like(carry_ref)

    is_p = i < RP // TS
    sc = 1.0 + jnp.where(is_p, scp_ref[...], scs_ref[...])
    sh = jnp.where(is_p, shp_ref[...], shs_ref[...])
    g = g_ref[...]
    for s in range(TS // SUB):
        rows = slice(s * SUB, (s + 1) * SUB)
        h_ref[rows, :] = _rms(x_ref[rows, :]) * g * sc + sh

    h = h_ref[...]
    h_hi = h.astype(BF16)
    hb_ref[...] = h_hi
    h_lo = (h - h_hi.astype(F32)).astype(BF16)
    nt = (((1,), (1,)), ((), ()))
    logits = (lax.dot_general(wrh_ref[...], h_hi, nt, preferred_element_type=F32)
              + lax.dot_general(wrl_ref[...], h_hi, nt, preferred_element_type=F32)
              + lax.dot_general(wrh_ref[...], h_lo, nt, preferred_element_type=F32))
    scores = jax.nn.sigmoid(logits)
    biased = scores + rb_ref[...]

    ge = NE // NGRP
    g3 = biased.reshape(NGRP, ge, TM)
    idx3 = lax.broadcasted_iota(I32, (NGRP, ge, TM), 1)
    m1 = jnp.max(g3, axis=1, keepdims=True)
    first = jnp.min(jnp.where(g3 == m1, idx3, ge), axis=1, keepdims=True)
    m2 = jnp.max(jnp.where(idx3 == first, -jnp.inf, g3), axis=1, keepdims=True)
    gscore = (m1 + m2).reshape(NGRP, TM)

    gidx = lax.broadcasted_iota(I32, (NGRP, TM), 0)
    grank = jnp.zeros((NGRP, TM), F32)
    for j in range(NGRP):
        sj = gscore[j:j + 1, :]
        grank = grank + jnp.where((sj > gscore) | ((sj == gscore) & (j < gidx)), 1.0, 0.0)
    gkeep = grank < TOPG
    ekeep = jnp.broadcast_to(gkeep.reshape(NGRP, 1, TM), (NGRP, ge, TM)).reshape(NE, TM)
    masked = jnp.where(ekeep, biased, -jnp.inf)

    eidx = lax.broadcasted_iota(I32, (NE, TM), 0)
    erank = jnp.zeros((NE, TM), F32)
    for j in range(NE):
        sj = masked[j:j + 1, :]
        erank = erank + jnp.where((sj > masked) | ((sj == masked) & (j < eidx)), 1.0, 0.0)
    sel = erank < TOPK

    sw = jnp.where(sel, scores, 0.0)
    cw = sw / jnp.sum(sw, axis=0, keepdims=True) * ROUTE_SCALE

    self_ = jnp.where(sel, 1.0, 0.0)
    t_src = lax.broadcasted_iota(I32, (TS, TS), 0)
    t_dst = lax.broadcasted_iota(I32, (TS, TS), 1)
    before = jnp.where(t_src < t_dst, 1.0, 0.0).astype(BF16)
    rank = jnp.dot(self_.astype(BF16), before, preferred_element_type=F32)
    n8 = jnp.floor((jnp.sum(self_, axis=1, keepdims=True) + (SUBLANE - 1)) / SUBLANE) * SUBLANE
    n8b = jnp.broadcast_to(n8, (NE, LANE))
    e_src = lax.broadcasted_iota(I32, (NE, NE), 1)
    e_dst = lax.broadcasted_iota(I32, (NE, NE), 0)
    lower = jnp.where(e_src < e_dst, 1.0, 0.0).astype(BF16)
    segoff = jnp.dot(lower, n8b.astype(BF16), preferred_element_type=F32)
    slot_dense = segoff[:, 0:1] + rank

    lane = lax.broadcasted_iota(I32, (NE, LANE), 1)
    tileoff = jnp.broadcast_to(carry_ref[...], (NE, LANE))
    meta_ref[...] = jnp.where(lane == 0, n8b, jnp.where(lane == 1, segoff, tileoff)).astype(I32)
    carry_ref[...] += n8
    cnt_ref[...] = jnp.broadcast_to(carry_ref[...], cnt_ref.shape).astype(I32)

    rem = sel
    for k in range(TOPK):
        ek = jnp.min(jnp.where(rem, eidx, NE), axis=0, keepdims=True)
        hit = eidx == ek
        slot_ref[k:k + 1, :] = jnp.sum(jnp.where(hit, slot_dense, 0.0), axis=0, keepdims=True).astype(I32)
        w_ref[k:k + 1, :] = jnp.sum(jnp.where(hit, cw, 0.0), axis=0, keepdims=True)
        rem = rem & jnp.logical_not(hit)


def _route(x, g, ada_p, ada_s, wr_hi, wr_lo, rbias):
    scp, scs = _ada_specs(4, TS)
    shp, shs = _ada_specs(3, TS)
    tok_spec = pl.BlockSpec((TOPK, TS), lambda i: (0, i))
    return pl.pallas_call(
        _route_kernel,
        grid=(NTS,),
        in_specs=[
            pl.BlockSpec((TS, D), lambda i: (i, 0)), _const_spec((1, D)), scp, scs, shp, shs,
            _const_spec((NE, D)), _const_spec((NE, D)), _const_spec((NE, 1)),
        ],
        out_specs=[pl.BlockSpec((TS, D), lambda i: (i, 0)), tok_spec, tok_spec,
                   pl.BlockSpec((NE, LANE), lambda i: (i, 0)), _const_spec((NE, LANE))],
        out_shape=[
            jax.ShapeDtypeStruct((R, D), BF16),
            jax.ShapeDtypeStruct((TOPK, R), I32),
            jax.ShapeDtypeStruct((TOPK, R), F32),
            jax.ShapeDtypeStruct((NTS * NE, LANE), I32),
            jax.ShapeDtypeStruct((NE, LANE), I32),
        ],
        scratch_shapes=[pltpu.VMEM((TS, D), F32), pltpu.VMEM((NE, 1), F32)],
        compiler_params=_cp("arbitrary"),
        name="route",
    )(x, g, ada_p, ada_s, ada_p, ada_s, wr_hi, wr_lo, rbias)


U32 = jnp.uint32
HI_MASK = 0xFFFF0000
SEG_CHUNK = 256


def _pack_pair(lo, hi):
    lo_bits = lax.shift_right_logical(lax.bitcast_convert_type(lo, U32), jnp.uint32(16))
    hi_bits = lax.bitcast_convert_type(hi, U32) & jnp.uint32(HI_MASK)
    return hi_bits | lo_bits


def _unpack_pair(words):
    lo = lax.bitcast_convert_type(lax.shift_left(words, jnp.uint32(16)), F32)
    hi = lax.bitcast_convert_type(words & jnp.uint32(HI_MASK), F32)
    return lo.astype(BF16), hi.astype(BF16)


def _round_bf16(x):
    return x.astype(BF16).astype(F32)


def _aligned(start, size):
    hint = lambda v: v if isinstance(v, int) else pl.multiple_of(v, SUBLANE)
    return pl.ds(hint(start), hint(size))


def _segment_copies(tile, n8_ref, seg_ref, toff_ref, base_ref, make_copy):
    def per_expert(e, carry):
        idx = tile * NE + e
        n8 = n8_ref[idx]
        tile_row = seg_ref[idx]
        global_row = base_ref[e] + toff_ref[idx]

        @pl.when(n8 > 0)
        def _():
            make_copy(tile_row, global_row, n8).start()

        return carry

    lax.fori_loop(0, NE, per_expert, 0)
    last = tile * NE + NE - 1
    return seg_ref[last] + n8_ref[last]


def _dispatch_kernel(n8_ref, seg_ref, toff_ref, base_ref, cnt_ref, h_ref, slot_ref, xb_ref,
                     s_ref, z_ref, sem, zsem):
    tile = pl.program_id(0)

    @pl.when(tile == 0)
    def _():
        z_ref[...] = jnp.zeros_like(z_ref)

        def per_expert(e, carry):
            n = cnt_ref[e]
            npad = (BLK - n % BLK) % BLK
            @pl.when(npad > 0)
            def _():
                cp = pltpu.make_async_copy(z_ref.at[_aligned(0, npad), :],
                                           xb_ref.at[_aligned(base_ref[e] + n, npad), :], zsem)
                cp.start()
                cp.wait()

            return carry

        lax.fori_loop(0, NE, per_expert, 0)

    h = h_ref[...]
    for c in range(SEG // SEG_CHUNK):
        rows = c * SEG_CHUNK + lax.broadcasted_iota(I32, (SEG_CHUNK, TS), 0)
        onehot = jnp.zeros((SEG_CHUNK, TS), F32)
        for k in range(TOPK):
            onehot = jnp.where(rows == slot_ref[k:k + 1, :], 1.0, onehot)
        s = jnp.dot(onehot.astype(BF16), h, preferred_element_type=F32)
        s_ref[c * SEG_CHUNK:(c + 1) * SEG_CHUNK, :] = _pack_pair(s[:, :DH], s[:, DH:])

    def make_copy(tile_row, global_row, size):
        return pltpu.make_async_copy(s_ref.at[_aligned(tile_row, size), :],
                                     xb_ref.at[_aligned(global_row, size), :], sem)

    used = _segment_copies(tile, n8_ref, seg_ref, toff_ref, base_ref, make_copy)

    @pl.when(used > 0)
    def _():
        pltpu.make_async_copy(s_ref.at[_aligned(0, used), :], xb_ref.at[_aligned(0, used), :], sem).wait()


def _dispatch(meta, base8, cnt8, hb, slot):
    n8, seg, toff = meta
    return pl.pallas_call(
        _dispatch_kernel,
        grid_spec=pltpu.PrefetchScalarGridSpec(
            num_scalar_prefetch=5,
            grid=(NTS,),
            in_specs=[pl.BlockSpec((TS, D), lambda i, *_: (i, 0)),
                      pl.BlockSpec((TOPK, TS), lambda i, *_: (0, i))],
            out_specs=pl.BlockSpec(memory_space=pl.ANY),
            scratch_shapes=[pltpu.VMEM((SEG, DH), U32), pltpu.VMEM((BLK, DH), U32),
                            pltpu.SemaphoreType.DMA, pltpu.SemaphoreType.DMA],
        ),
        out_shape=jax.ShapeDtypeStruct((NROWS, DH), U32),
        compiler_params=_cp("arbitrary"),
        name="dispatch",
    )(n8, seg, toff, base8, cnt8, hb, slot)


def _experts_kernel(be_ref, nu_ref, xb_ref, w1_ref, w3_ref, w2_ref, yb_ref, w1b_ref, w3b_ref, w2b_ref):
    b = pl.program_id(0)
    live = b < nu_ref[0]
    fresh = (b == 0) | (be_ref[b] != be_ref[jnp.maximum(b - 1, 0)])

    @pl.when(live & fresh)
    def _():
        w1b_ref[...] = w1_ref[...].astype(BF16)
        w3b_ref[...] = w3_ref[...].astype(BF16)
        w2b_ref[...] = w2_ref[...].astype(BF16)

    @pl.when(live)
    def _():
        x_lo, x_hi = _unpack_pair(xb_ref[...])

        def up(w_ref):
            return (jnp.dot(x_lo, w_ref[:DH, :], preferred_element_type=F32)
                    + jnp.dot(x_hi, w_ref[DH:, :], preferred_element_type=F32))

        hid = (_silu(up(w1b_ref)) * up(w3b_ref)).astype(BF16)
        for c in range(DH // MXU_N):
            cols = slice(c * MXU_N, (c + 1) * MXU_N)
            y_lo = jnp.dot(hid, w2b_ref[:, c * MXU_N:(c + 1) * MXU_N], preferred_element_type=F32)
            y_hi = jnp.dot(hid, w2b_ref[:, DH + c * MXU_N:DH + (c + 1) * MXU_N], preferred_element_type=F32)
            yb_ref[:, cols] = _pack_pair(_round_bf16(y_lo), _round_bf16(y_hi))


def _experts(blk_e, n_used, xb, w1, w3, w2, layer):
    row_map = lambda b, be, nu: (jnp.minimum(b, nu[0] - 1), 0)
    w_map = lambda b, be, nu: (layer, be[b], 0, 0)
    return pl.pallas_call(
        _experts_kernel,
        grid_spec=pltpu.PrefetchScalarGridSpec(
            num_scalar_prefetch=2,
            grid=(NB,),
            in_specs=[
                pl.BlockSpec((BLK, DH), row_map),
                pl.BlockSpec((None, None, D, DE), w_map),
                pl.BlockSpec((None, None, D, DE), w_map),
                pl.BlockSpec((None, None, DE, D), w_map),
            ],
            out_specs=pl.BlockSpec((BLK, DH), row_map),
            scratch_shapes=[pltpu.VMEM((D, DE), BF16), pltpu.VMEM((D, DE), BF16), pltpu.VMEM((DE, D), BF16)],
        ),
        out_shape=jax.ShapeDtypeStruct((NROWS, DH), U32),
        compiler_params=_cp("arbitrary"),
        name="experts",
    )(blk_e, n_used, xb, w1, w3, w2)


def _shared_kernel(h_ref, w1_ref, w3_ref, w2_ref, o_ref):
    x = h_ref[...].astype(BF16)
    hid = (_silu(jnp.dot(x, w1_ref[...], preferred_element_type=F32))
           * jnp.dot(x, w3_ref[...], preferred_element_type=F32))
    o_ref[...] = jnp.dot(hid.astype(BF16), w2_ref[...], preferred_element_type=F32)


def _shared(h2, ws1, ws3, ws2):
    return pl.pallas_call(
        _shared_kernel,
        grid=(NT,),
        in_specs=[pl.BlockSpec((TM, D), lambda i: (i, 0)),
                  _const_spec((D, DE)), _const_spec((D, DE)), _const_spec((DE, D))],
        out_specs=pl.BlockSpec((TM, D), lambda i: (i, 0)),
        out_shape=jax.ShapeDtypeStruct((R, D), F32),
        compiler_params=_cp("arbitrary"),
        name="shared",
    )(h2, ws1, ws3, ws2)


def _combine_kernel(n8_ref, seg_ref, toff_ref, base_ref, x_ref, sh_ref, slot_ref, wt_ref, gp_ref, gs_ref,
                    gpost_ref, yb_ref, o_ref, ys_ref, f_ref, sem):
    tile = pl.program_id(0)

    @pl.when(tile == 0)
    def _():
        ys_ref[...] = jnp.zeros_like(ys_ref)

    def make_copy(tile_row, global_row, size):
        return pltpu.make_async_copy(yb_ref.at[_aligned(global_row, size), :],
                                     ys_ref.at[_aligned(tile_row, size), :], sem)

    used = _segment_copies(tile, n8_ref, seg_ref, toff_ref, base_ref, make_copy)

    cols = lax.broadcasted_iota(I32, (TS, SEG), 1)
    cmat = jnp.zeros((TS, SEG), F32)
    for k in range(TOPK):
        cmat = jnp.where(cols == slot_ref[:, k:k + 1], wt_ref[:, k:k + 1], cmat)
    cmat = cmat.astype(BF16)

    @pl.when(used > 0)
    def _():
        pltpu.make_async_copy(yb_ref.at[_aligned(0, used), :], ys_ref.at[_aligned(0, used), :], sem).wait()

    y_lo, y_hi = _unpack_pair(ys_ref[...])
    f_ref[:, :DH] = sh_ref[:, :DH] + jnp.dot(cmat, y_lo, preferred_element_type=F32)
    f_ref[:, DH:] = sh_ref[:, DH:] + jnp.dot(cmat, y_hi, preferred_element_type=F32)
    gate = jnp.where(tile < RP // TS, gp_ref[...], gs_ref[...])
    gpost = gpost_ref[...]
    for s in range(TS // SUB):
        rows = slice(s * SUB, (s + 1) * SUB)
        o_ref[rows, :] = x_ref[rows, :] + gate * (_rms(f_ref[rows, :]) * gpost)


def _combine(meta, base8, x, shared, slot_t, w_t, ada_p, ada_s, gpost, yb):
    n8, seg, toff = meta
    tiles_per_seq = LP // TS
    gp = pl.BlockSpec((None, 1, D), lambda i, *_: (jnp.minimum(i // tiles_per_seq, NP - 1), 0, 5))
    gs = pl.BlockSpec((NS, D), lambda i, *_: (0, 5))
    row = pl.BlockSpec((TS, D), lambda i, *_: (i, 0))
    tok = pl.BlockSpec((TS, TOPK), lambda i, *_: (i, 0))
    return pl.pallas_call(
        _combine_kernel,
        grid_spec=pltpu.PrefetchScalarGridSpec(
            num_scalar_prefetch=4,
            grid=(NTS,),
            in_specs=[row, row, tok, tok, gp, gs, pl.BlockSpec((1, D), lambda i, *_: (0, 0)),
                      pl.BlockSpec(memory_space=pl.ANY)],
            out_specs=row,
            scratch_shapes=[pltpu.VMEM((SEG, DH), U32), pltpu.VMEM((TS, D), F32), pltpu.SemaphoreType.DMA],
        ),
        out_shape=jax.ShapeDtypeStruct((R, D), F32),
        compiler_params=_cp("arbitrary"),
        name="combine",
    )(n8, seg, toff, base8, x, shared, slot_t, w_t, ada_p, ada_s, gpost, yb)


def _row(v):
    return v.reshape(1, -1)


def _to_rows(x_prompt, x_sample):
    return jnp.concatenate([x_prompt.reshape(RP, D), x_sample.transpose(1, 0, 2).reshape(RS, D)], axis=0)


def _sample_rows_to_batch(rows, width):
    return rows.reshape(LS, NS, width).transpose(1, 0, 2)


def kernel(x_prompt, x_sample, state_conv, state_pool, c_prompt, c_sample, w_ada, b_ada, g_pre_mix,
           g_post_mix, g_pre_ffn, g_post_ffn, w_in, conv_w, conv_b, conv_ln_g, conv_ln_b, conv_w_pw,
           gmlp_ln_g, gmlp_ln_b, gmlp_w_s, gmlp_b_s, gmlp_w_o, pool_w, pool_scale, w_out, router_w,
           router_bias, w1, w3, w2, ws1, ws3, ws2):
    x = _to_rows(x_prompt, x_sample)
    ada = _ada(jnp.concatenate([c_prompt, c_sample], axis=0), w_ada, b_ada)

    conv_p, pool_p, conv_s, pool_s, v_s = [], [], [], [], []
    for l in range(DEPTH):
        ada_p = ada[l, :NP].reshape(NP, 1, 6 * D)
        ada_s = ada[l, NP:]

        h = _hmod(x, _row(g_pre_mix[l]), ada_p, ada_s, 1, 0)
        glu = _colmm(h, w_in, l, (0, DC), DC, 512, _glu_epilogue, F32, "in_glu")
        zb = _colmm(h, w_in, l, (COL_GMLP,), 2 * DC, 1024, _gelu_epilogue, F32, "in_gelu")
        p = _colmm(h, w_in, l, (COL_POOL,), DC, 1024, _id_epilogue, F32, "in_pool")
        gates = _colmm(h, w_in, l, (COL_GATE,), 3 * D, 1024, _sigmoid_epilogue, BF16, "in_gate")

        cw, cb = conv_w[l], _row(conv_b[l])
        clg, clb = _row(conv_ln_g[l]), _row(conv_ln_b[l])
        a_p = _conv_p(glu, cw, cb, clg, clb)
        a_s = _conv_s(glu, state_conv[l].transpose(1, 0, 2), cw, cb, clg, clb)

        glg, glb = _row(gmlp_ln_g[l]), _row(gmlp_ln_b[l])
        b_p = _gmlp_p(zb, gmlp_w_s[l], gmlp_b_s[l].T, glg, glb)
        coef = jnp.repeat(gmlp_w_s[l][:, :LS, :LS].transpose(1, 2, 0).reshape(LS * LS, HEADS), DC // HEADS, axis=1)
        bias = jnp.repeat(gmlp_b_s[l][:, :LS].T, DC // HEADS, axis=1)
        b_s, v_rows = _gmlp_s(zb, coef, bias, glg, glb)

        c_p = _pool_p(p)
        c_s = _pool_s(p, state_pool[l].transpose(1, 0, 2))

        x = _merge(((a_p, a_s), (b_p, b_s), (c_p, c_s)), gates, x, ada_p, ada_s, _row(g_post_mix[l]),
                   conv_w_pw[l].astype(BF16), gmlp_w_o[l].astype(BF16), pool_w[l].astype(BF16),
                   _row(pool_scale[l]), w_out[l].astype(BF16))

        glu_p = glu[:RP].reshape(NP, LP, DC)
        p_p = p[:RP].reshape(NP, LP, DC)
        conv_p.append(glu_p[:, LP - HIST_C:])
        pool_p.append(p_p[:, LP - HIST_P:])
        conv_s.append(jnp.concatenate([state_conv[l][:, LS:], _sample_rows_to_batch(glu[RP:], DC)], axis=1))
        pool_s.append(jnp.concatenate([state_pool[l][:, LS:], _sample_rows_to_batch(p[RP:], DC)], axis=1))
        v_s.append(_sample_rows_to_batch(v_rows, DC))

        wr_t = router_w[l].T
        wr_hi = wr_t.astype(BF16)
        wr_lo = (wr_t - wr_hi.astype(F32)).astype(BF16)
        hb, slot, w_k, meta, cnt = _route(x, _row(g_pre_ffn[l]), ada_p, ada_s, wr_hi, wr_lo,
                                          router_bias[l].reshape(NE, 1))
        meta = tuple(meta[:, c] for c in range(3))
        cnt8 = cnt[:, 0]
        padded = (cnt8 + BLK - 1) // BLK * BLK
        pad_end = jnp.cumsum(padded)
        base8 = (pad_end - padded).astype(I32)
        n_used = (pad_end[-1] // BLK).astype(I32)
        blk = jnp.minimum(jnp.arange(NB, dtype=I32), n_used - 1)
        blk_e = jnp.minimum(jnp.sum((pad_end[None, :] <= (blk * BLK)[:, None]).astype(I32), axis=1), NE - 1)

        xb = _dispatch(meta, base8, cnt8, hb, slot)
        yb = _experts(blk_e, n_used.reshape(1), xb, w1, w3, w2, l)
        shared = _shared(hb, ws1[l].astype(BF16), ws3[l].astype(BF16), ws2[l].astype(BF16))
        x = _combine(meta, base8, x, shared, slot.T, w_k.T, ada_p, ada_s, _row(g_post_ffn[l]), yb)

    y_prompt = x[:RP].reshape(NP, LP, D)
    y_sample = _sample_rows_to_batch(x[RP:], D)
    return (y_prompt, y_sample, jnp.stack(conv_p), jnp.stack(pool_p), jnp.stack(conv_s),
            jnp.stack(pool_s), jnp.stack(v_s))
```

```python
import functools

import jax
import jax.numpy as jnp
from jax import lax
from jax.experimental import pallas as pl
from jax.experimental.pallas import tpu as pltpu

F32 = jnp.float32
BF16 = jnp.bfloat16
I32 = jnp.int32

D = 2048
NP, LP = 4, 2048
NS, LS = 128, 4
RP = NP * LP
RS = NS * LS
R = RP + RS
DEPTH = 2
DC = 1024
CONV_W = 31
HIST_C = CONV_W - 1
HIST_P = 15
POOL_WINDOWS = (2, 4, 8, 16)
HEADS = 8
CHUNK = 128
D_IN = 5 * DC + 3 * D
COL_GMLP, COL_POOL, COL_GATE = 2 * DC, 4 * DC, 5 * DC
NE, TOPK, NGRP, TOPG = 64, 8, 8, 4
DE = 512
ROUTE_SCALE = 2.5
EPS = 1e-6

TM = 512
NT = R // TM
NTP = RP // TM
SUB = 128
NSUB = TM // SUB
TMM = 256
LANE = 128
SUBLANE = 8
MXU_N = 256
TS = 256
NTS = R // TS
SEG = TOPK * TS + NE * SUBLANE
DH = D // 2
BLK = 256
NB = -(-(R * TOPK + NTS * NE * (SUBLANE - 1) + NE * (BLK - 1)) // BLK)
NROWS = NB * BLK

VMEM_LIMIT = 56 * 1024 * 1024


def _cp(*sem):
    return pltpu.CompilerParams(dimension_semantics=sem, vmem_limit_bytes=VMEM_LIMIT)


def _const_spec(shape):
    nd = len(shape)
    return pl.BlockSpec(shape, lambda *_: (0,) * nd, pipeline_mode=pl.Buffered(1))


def _rms(x):
    return x * lax.rsqrt(jnp.mean(x * x, axis=-1, keepdims=True) + EPS)


def _ln(x, g, b):
    mu = jnp.mean(x, axis=-1, keepdims=True)
    xc = x - mu
    var = jnp.mean(xc * xc, axis=-1, keepdims=True)
    return xc * lax.rsqrt(var + EPS) * g + b


def _silu(x):
    return x * jax.nn.sigmoid(x)


ADA_TN = 1024


def _ada_kernel(c_ref, w_ref, b_ref, o_ref):
    c = _silu(c_ref[...]).astype(BF16)
    o_ref[...] = jnp.dot(c, w_ref[...].astype(BF16), preferred_element_type=F32) + b_ref[...]


def _ada(c_all, w_ada, b_ada):
    n = c_all.shape[0]
    return pl.pallas_call(
        _ada_kernel,
        grid=(DEPTH, 6 * D // ADA_TN),
        in_specs=[
            pl.BlockSpec((n, D), lambda l, j: (0, 0)),
            pl.BlockSpec((None, D, ADA_TN), lambda l, j: (l, 0, j)),
            pl.BlockSpec((None, 1, ADA_TN), lambda l, j: (l, 0, j)),
        ],
        out_specs=pl.BlockSpec((None, n, ADA_TN), lambda l, j: (l, 0, j)),
        out_shape=jax.ShapeDtypeStruct((DEPTH, n, 6 * D), F32),
        compiler_params=_cp("arbitrary", "arbitrary"),
        name="ada",
    )(c_all, w_ada, b_ada.reshape(DEPTH, 1, 6 * D))


def _ada_specs(chunk, tile_rows):
    tiles_per_seq = LP // tile_rows
    p = pl.BlockSpec((None, 1, D), lambda i: (jnp.minimum(i // tiles_per_seq, NP - 1), 0, chunk))
    s = pl.BlockSpec((NS, D), lambda i: (0, chunk))
    return p, s


def _hmod_kernel(x_ref, g_ref, scp_ref, scs_ref, shp_ref, shs_ref, o_ref):
    is_p = pl.program_id(0) < NTP
    sc = 1.0 + jnp.where(is_p, scp_ref[...], scs_ref[...])
    sh = jnp.where(is_p, shp_ref[...], shs_ref[...])
    g = g_ref[...]
    for s in range(NSUB):
        rows = slice(s * SUB, (s + 1) * SUB)
        o_ref[rows, :] = (_rms(x_ref[rows, :]) * g * sc + sh).astype(o_ref.dtype)


def _hmod(x, g, ada_p, ada_s, c_scale, c_shift):
    scp, scs = _ada_specs(c_scale, TM)
    shp, shs = _ada_specs(c_shift, TM)
    return pl.pallas_call(
        _hmod_kernel,
        grid=(NT,),
        in_specs=[pl.BlockSpec((TM, D), lambda i: (i, 0)), _const_spec((1, D)), scp, scs, shp, shs],
        out_specs=pl.BlockSpec((TM, D), lambda i: (i, 0)),
        out_shape=jax.ShapeDtypeStruct((R, D), BF16),
        compiler_params=_cp("arbitrary"),
        name="hmod",
    )(x, g, ada_p, ada_s, ada_p, ada_s)


def _colmm_kernel(nw, epilogue, *refs):
    lhs_ref = refs[0]
    w_refs = refs[1:1 + nw]
    o_ref = refs[1 + nw]
    wb_refs = refs[2 + nw:]

    @pl.when(pl.program_id(1) == 0)
    def _():
        for w_ref, wb_ref in zip(w_refs, wb_refs):
            wb_ref[...] = w_ref[...].astype(BF16)

    lhs = lhs_ref[...]
    accs = [jnp.dot(lhs, wb_ref[...], preferred_element_type=F32) for wb_ref in wb_refs]
    o_ref[...] = epilogue(*accs).astype(o_ref.dtype)


def _colmm(h, w_in, layer, col_offsets, n_cols, tn, epilogue, out_dtype, name):
    nw = len(col_offsets)
    w_specs = [
        pl.BlockSpec((None, D, tn), functools.partial(lambda j, i, o: (layer, 0, o + j), o=off // tn))
        for off in col_offsets
    ]
    return pl.pallas_call(
        functools.partial(_colmm_kernel, nw, epilogue),
        grid=(n_cols // tn, NT),
        in_specs=[pl.BlockSpec((TM, D), lambda j, i: (i, 0))] + w_specs,
        out_specs=pl.BlockSpec((TM, tn), lambda j, i: (i, j)),
        out_shape=jax.ShapeDtypeStruct((R, n_cols), out_dtype),
        scratch_shapes=[pltpu.VMEM((D, tn), BF16) for _ in range(nw)],
        compiler_params=_cp("arbitrary", "arbitrary"),
        name=name,
    )(h, *([w_in] * nw))


def _glu_epilogue(a, g):
    return a * jax.nn.sigmoid(g)


def _gelu_epilogue(z):
    return 0.5 * z * (1.0 + lax.erf(z * (0.5 ** 0.5)))


def _id_epilogue(z):
    return z


def _sigmoid_epilogue(z):
    return jax.nn.sigmoid(z)


CONV_HALO = 32


def _conv_taps(w_ref, slab, cs):
    acc = None
    for k in range(CONV_W):
        term = w_ref[k:k + 1, cs] * slab(k)
        acc = term if acc is None else acc + term
    return acc


def _conv_p_kernel(main_ref, halo_ref, w_ref, b_ref, lg_ref, lb_ref, o_ref, xs_ref, y_ref):
    first = (pl.program_id(0) % (LP // SUB)) == 0
    xs_ref[0, 0:CONV_HALO, :] = jnp.where(first, 0.0, halo_ref[...])
    xs_ref[0, CONV_HALO:, :] = main_ref[...]
    shift_rows = CONV_HALO + SUB - SUBLANE
    for s in range(1, SUBLANE):
        xs_ref[s, 0:shift_rows, :] = xs_ref[0, s:s + shift_rows, :]
    base = CONV_HALO - HIST_C

    def window(k, cs):
        s = (base + k) % SUBLANE
        start = base + k - s
        return xs_ref[s, start:start + SUB, cs]

    for c in range(DC // LANE):
        cs = slice(c * LANE, (c + 1) * LANE)
        acc = _conv_taps(w_ref, functools.partial(window, cs=cs), cs)
        y_ref[:, cs] = acc + b_ref[:, cs]
    o_ref[...] = _silu(_ln(y_ref[...], lg_ref[...], lb_ref[...])).astype(o_ref.dtype)


def _conv_p(glu, w, b, lg, lb):
    halo_per_tile = SUB // CONV_HALO
    return pl.pallas_call(
        _conv_p_kernel,
        grid=(RP // SUB,),
        in_specs=[
            pl.BlockSpec((SUB, DC), lambda i: (i, 0)),
            pl.BlockSpec((CONV_HALO, DC), lambda i: (jnp.maximum(i * halo_per_tile - 1, 0), 0)),
            _const_spec((CONV_W, DC)), _const_spec((1, DC)), _const_spec((1, DC)), _const_spec((1, DC)),
        ],
        out_specs=pl.BlockSpec((SUB, DC), lambda i: (i, 0)),
        out_shape=jax.ShapeDtypeStruct((RP, DC), BF16),
        scratch_shapes=[pltpu.VMEM((SUBLANE, CONV_HALO + SUB, DC), F32), pltpu.VMEM((SUB, DC), F32)],
        compiler_params=_cp("arbitrary"),
        name="conv_prompt",
    )(glu, glu, w, b, lg, lb)


def _conv_s_kernel(glu_ref, st_ref, w_ref, b_ref, lg_ref, lb_ref, o_ref, y_ref):
    for l in range(LS):
        rows = slice(l * SUB, (l + 1) * SUB)

        def slab(k, cs):
            j = l + k
            if j < HIST_C:
                return st_ref[j, :, cs]
            return glu_ref[(j - HIST_C) * SUB:(j - HIST_C + 1) * SUB, cs]

        for c in range(DC // LANE):
            cs = slice(c * LANE, (c + 1) * LANE)
            acc = _conv_taps(w_ref, functools.partial(slab, cs=cs), cs)
            y_ref[:, cs] = acc + b_ref[:, cs]
        o_ref[rows, :] = _silu(_ln(y_ref[...], lg_ref[...], lb_ref[...])).astype(o_ref.dtype)


def _conv_s(glu, state_t, w, b, lg, lb):
    return pl.pallas_call(
        _conv_s_kernel,
        grid=(1,),
        in_specs=[
            pl.BlockSpec((RS, DC), lambda i: (RP // RS, 0)),
            _const_spec((HIST_C, NS, DC)),
            _const_spec((CONV_W, DC)), _const_spec((1, DC)), _const_spec((1, DC)), _const_spec((1, DC)),
        ],
        out_specs=_const_spec((RS, DC)),
        out_shape=jax.ShapeDtypeStruct((RS, DC), BF16),
        scratch_shapes=[pltpu.VMEM((SUB, DC), F32)],
        compiler_params=_cp("arbitrary"),
        name="conv_sample",
    )(glu, state_t, w, b, lg, lb)


def _gmlp_p_kernel(u_ref, v_ref, ws_ref, bst_ref, lg_ref, lb_ref, o_ref):
    v = _ln(v_ref[...], lg_ref[...], lb_ref[...]).astype(BF16)
    tgt = lax.broadcasted_iota(I32, (CHUNK, CHUNK), 0)
    src = lax.broadcasted_iota(I32, (CHUNK, CHUNK), 1)
    causal = src <= tgt
    hd = DC // HEADS
    for h in range(HEADS):
        cs = slice(h * hd, (h + 1) * hd)
        ws = jnp.where(causal, ws_ref[h], 0.0).astype(BF16)
        mixed = jnp.dot(ws, v[:, cs], preferred_element_type=F32) + bst_ref[:, h:h + 1]
        o_ref[:, cs] = (u_ref[:, cs] * mixed).astype(o_ref.dtype)


def _gmlp_p(zb, ws, bst, lg, lb):
    return pl.pallas_call(
        _gmlp_p_kernel,
        grid=(RP // CHUNK,),
        in_specs=[
            pl.BlockSpec((CHUNK, DC), lambda i: (i, 0)),
            pl.BlockSpec((CHUNK, DC), lambda i: (i, 1)),
            _const_spec((HEADS, CHUNK, CHUNK)), _const_spec((CHUNK, HEADS)),
            _const_spec((1, DC)), _const_spec((1, DC)),
        ],
        out_specs=pl.BlockSpec((CHUNK, DC), lambda i: (i, 0)),
        out_shape=jax.ShapeDtypeStruct((RP, DC), BF16),
        compiler_params=_cp("arbitrary"),
        name="gmlp_prompt",
    )(zb, zb, ws, bst, lg, lb)


def _gmlp_s_kernel(u_ref, v_ref, coef_ref, bias_ref, lg_ref, lb_ref, o_ref, vo_ref):
    for l in range(LS):
        rows = slice(l * SUB, (l + 1) * SUB)
        vo_ref[rows, :] = _ln(v_ref[rows, :], lg_ref[...], lb_ref[...])
    for l in range(LS):
        rows = slice(l * SUB, (l + 1) * SUB)
        mixed = bias_ref[l:l + 1, :]
        for s in range(l + 1):
            mixed = mixed + coef_ref[l * LS + s:l * LS + s + 1, :] * vo_ref[s * SUB:(s + 1) * SUB, :]
        o_ref[rows, :] = (u_ref[rows, :] * mixed).astype(o_ref.dtype)


def _gmlp_s(zb, coef, bias, lg, lb):
    return pl.pallas_call(
        _gmlp_s_kernel,
        grid=(1,),
        in_specs=[
            pl.BlockSpec((RS, DC), lambda i: (RP // RS, 0)),
            pl.BlockSpec((RS, DC), lambda i: (RP // RS, 1)),
            _const_spec((LS * LS, DC)), _const_spec((LS, DC)), _const_spec((1, DC)), _const_spec((1, DC)),
        ],
        out_specs=[_const_spec((RS, DC)), _const_spec((RS, DC))],
        out_shape=[jax.ShapeDtypeStruct((RS, DC), BF16), jax.ShapeDtypeStruct((RS, DC), F32)],
        compiler_params=_cp("arbitrary"),
        name="gmlp_sample",
    )(zb, zb, coef, bias, lg, lb)


POOL_HALO = 16
GD = DC // len(POOL_WINDOWS)


def _pool_p_kernel(main_ref, halo_ref, o_ref, xc_ref):
    t_tile = pl.program_id(0) % (LP // SUB)
    xc_ref[0:POOL_HALO, :] = jnp.where(t_tile == 0, 0.0, halo_ref[...])
    xc_ref[POOL_HALO:, :] = main_ref[...]
    pos = t_tile * SUB + lax.broadcasted_iota(I32, (SUB, 1), 0)
    for g, w in enumerate(POOL_WINDOWS):
        cs = slice(g * GD, (g + 1) * GD)
        tok = xc_ref[POOL_HALO:POOL_HALO + SUB, cs]
        acc = tok
        for j in range(1, w):
            acc = acc + xc_ref[POOL_HALO - j:POOL_HALO - j + SUB, cs]
        cnt = jnp.minimum(pos + 1, w).astype(F32)
        o_ref[:, cs] = (acc / cnt - tok).astype(o_ref.dtype)


def _pool_p(p):
    halo_per_tile = SUB // POOL_HALO
    return pl.pallas_call(
        _pool_p_kernel,
        grid=(RP // SUB,),
        in_specs=[
            pl.BlockSpec((SUB, DC), lambda i: (i, 0)),
            pl.BlockSpec((POOL_HALO, DC), lambda i: (jnp.maximum(i * halo_per_tile - 1, 0), 0)),
        ],
        out_specs=pl.BlockSpec((SUB, DC), lambda i: (i, 0)),
        out_shape=jax.ShapeDtypeStruct((RP, DC), BF16),
        scratch_shapes=[pltpu.VMEM((POOL_HALO + SUB, DC), F32)],
        compiler_params=_cp("arbitrary"),
        name="pool_prompt",
    )(p, p)


def _pool_s_kernel(p_ref, st_ref, o_ref):
    def slab(j, cs):
        if j < HIST_P:
            return st_ref[j, :, cs]
        return p_ref[(j - HIST_P) * SUB:(j - HIST_P + 1) * SUB, cs]

    for l in range(LS):
        for g, w in enumerate(POOL_WINDOWS):
            cs = slice(g * GD, (g + 1) * GD)
            tok = slab(HIST_P + l, cs)
            acc = tok
            for j in range(1, w):
                acc = acc + slab(HIST_P + l - j, cs)
            o_ref[l * SUB:(l + 1) * SUB, cs] = (acc / float(w) - tok).astype(o_ref.dtype)


def _pool_s(p, state_t):
    return pl.pallas_call(
        _pool_s_kernel,
        grid=(1,),
        in_specs=[pl.BlockSpec((RS, DC), lambda i: (RP // RS, 0)), _const_spec((HIST_P, NS, DC))],
        out_specs=_const_spec((RS, DC)),
        out_shape=jax.ShapeDtypeStruct((RS, DC), BF16),
        compiler_params=_cp("arbitrary"),
        name="pool_sample",
    )(p, state_t)


def _merge_kernel(ap_ref, as_ref, bp_ref, bs_ref, cp_ref, cs_ref, ga_ref, gb_ref, gc_ref, x_ref,
                  gp_ref, gs_ref, gpost_ref, wpw_ref, wo_ref, wpool_ref, pscale_ref, wout_ref,
                  o_ref, m_ref):
    is_p = pl.program_id(0) < RP // TMM
    a = jnp.where(is_p, ap_ref[...], as_ref[...])
    b = jnp.where(is_p, bp_ref[...], bs_ref[...])
    c = jnp.where(is_p, cp_ref[...], cs_ref[...])
    m_ref[...] = ga_ref[...].astype(F32) * jnp.dot(a, wpw_ref[...], preferred_element_type=F32)
    m_ref[...] += gb_ref[...].astype(F32) * jnp.dot(b, wo_ref[...], preferred_element_type=F32)
    eo = D // len(POOL_WINDOWS)
    for g in range(len(POOL_WINDOWS)):
        os_ = slice(g * eo, (g + 1) * eo)
        yc = jnp.dot(c[:, g * GD:(g + 1) * GD], wpool_ref[g], preferred_element_type=F32)
        m_ref[:, os_] += gc_ref[:, os_].astype(F32) * (yc * pscale_ref[:, os_])
    mix = jnp.dot(m_ref[...].astype(BF16), wout_ref[...], preferred_element_type=F32)
    m_ref[...] = _rms(mix) * gpost_ref[...]
    gate = jnp.where(is_p, gp_ref[...], gs_ref[...])
    for s in range(TMM // SUB):
        rows = slice(s * SUB, (s + 1) * SUB)
        o_ref[rows, :] = x_ref[rows, :] + gate * m_ref[rows, :]


def _merge(acts, gates, x, ada_p, ada_s, gpost, wpw, wo, wpool, pscale, wout):
    (a_p, a_s), (b_p, b_s), (c_p, c_s) = acts
    ntp = RP // TMM
    pspec = pl.BlockSpec((TMM, DC), lambda i: (jnp.minimum(i, ntp - 1), 0))
    sspec = pl.BlockSpec((TMM, DC), lambda i: (jnp.maximum(i - ntp, 0), 0))
    gp, gs = _ada_specs(2, TMM)
    return pl.pallas_call(
        _merge_kernel,
        grid=(R // TMM,),
        in_specs=[
            pspec, sspec, pspec, sspec, pspec, sspec,
            pl.BlockSpec((TMM, D), lambda i: (i, 0)),
            pl.BlockSpec((TMM, D), lambda i: (i, 1)),
            pl.BlockSpec((TMM, D), lambda i: (i, 2)),
            pl.BlockSpec((TMM, D), lambda i: (i, 0)),
            gp, gs, _const_spec((1, D)),
            _const_spec((DC, D)), _const_spec((DC, D)), _const_spec((len(POOL_WINDOWS), GD, D // 4)),
            _const_spec((1, D)), _const_spec((D, D)),
        ],
        out_specs=pl.BlockSpec((TMM, D), lambda i: (i, 0)),
        out_shape=jax.ShapeDtypeStruct((R, D), F32),
        scratch_shapes=[pltpu.VMEM((TMM, D), F32)],
        compiler_params=_cp("arbitrary"),
        name="merge",
    )(a_p, a_s, b_p, b_s, c_p, c_s, gates, gates, gates, x, ada_p, ada_s, gpost,
      wpw, wo, wpool, pscale, wout)


def _route_kernel(x_ref, g_ref, scp_ref, scs_ref, shp_ref, shs_ref, wrh_ref, wrl_ref, rb_ref,
                  hb_ref, slot_ref, w_ref, meta_ref, cnt_ref, h_ref, carry_ref):
    TM = TS
    i = pl.program_id(0)

    @pl.when(i == 0)
    def _():
        carry_ref[...] = jnp.zeros_like(carry_ref)

    is_p = i < RP // TS
    sc = 1.0 + jnp.where(is_p, scp_ref[...], scs_ref[...])
    sh = jnp.where(is_p, shp_ref[...], shs_ref[...])
    g = g_ref[...]
    for s in range(TS // SUB):
        rows = slice(s * SUB, (s + 1) * SUB)
        h_ref[rows, :] = _rms(x_ref[rows, :]) * g * sc + sh

    h = h_ref[...]
    h_hi = h.astype(BF16)
    hb_ref[...] = h_hi
    h_lo = (h - h_hi.astype(F32)).astype(BF16)
    nt = (((1,), (1,)), ((), ()))
    logits = (lax.dot_general(wrh_ref[...], h_hi, nt, preferred_element_type=F32)
              + lax.dot_general(wrl_ref[...], h_hi, nt, preferred_element_type=F32)
              + lax.dot_general(wrh_ref[...], h_lo, nt, preferred_element_type=F32))
    scores = jax.nn.sigmoid(logits)
    biased = scores + rb_ref[...]

    ge = NE // NGRP
    g3 = biased.reshape(NGRP, ge, TM)
    idx3 = lax.broadcasted_iota(I32, (NGRP, ge, TM), 1)
    m1 = jnp.max(g3, axis=1, keepdims=True)
    first = jnp.min(jnp.where(g3 == m1, idx3, ge), axis=1, keepdims=True)
    m2 = jnp.max(jnp.where(idx3 == first, -jnp.inf, g3), axis=1, keepdims=True)
    gscore = (m1 + m2).reshape(NGRP, TM)

    gidx = lax.broadcasted_iota(I32, (NGRP, TM), 0)
    grank = jnp.zeros((NGRP, TM), F32)
    for j in range(NGRP):
        sj = gscore[j:j + 1, :]
        grank = grank + jnp.where((sj > gscore) | ((sj == gscore) & (j < gidx)), 1.0, 0.0)
    gkeep = grank < TOPG
    ekeep = jnp.broadcast_to(gkeep.reshape(NGRP, 1, TM), (NGRP, ge, TM)).reshape(NE, TM)
    masked = jnp.where(ekeep, biased, -jnp.inf)

    eidx = lax.broadcasted_iota(I32, (NE, TM), 0)
    erank = jnp.zeros((NE, TM), F32)
    for j in range(NE):
        sj = masked[j:j + 1, :]
        erank = erank + jnp.where((sj > masked) | ((sj == masked) & (j < eidx)), 1.0, 0.0)
    sel = erank < TOPK

    sw = jnp.where(sel, scores, 0.0)
    cw = sw / jnp.sum(sw, axis=0, keepdims=True) * ROUTE_SCALE

    self_ = jnp.where(sel, 1.0, 0.0)
    t_src = lax.broadcasted_iota(I32, (TS, TS), 0)
    t_dst = lax.broadcasted_iota(I32, (TS, TS), 1)
    before = jnp.where(t_src < t_dst, 1.0, 0.0).astype(BF16)
    rank = jnp.dot(self_.astype(BF16), before, preferred_element_type=F32)
    n8 = jnp.floor((jnp.sum(self_, axis=1, keepdims=True) + (SUBLANE - 1)) / SUBLANE) * SUBLANE
    n8b = jnp.broadcast_to(n8, (NE, LANE))
    e_src = lax.broadcasted_iota(I32, (NE, NE), 1)
    e_dst = lax.broadcasted_iota(I32, (NE, NE), 0)
    lower = jnp.where(e_src < e_dst, 1.0, 0.0).astype(BF16)
    segoff = jnp.dot(lower, n8b.astype(BF16), preferred_element_type=F32)
    slot_dense = segoff[:, 0:1] + rank

    lane = lax.broadcasted_iota(I32, (NE, LANE), 1)
    tileoff = jnp.broadcast_to(carry_ref[...], (NE, LANE))
    meta_ref[...] = jnp.where(lane == 0, n8b, jnp.where(lane == 1, segoff, tileoff)).astype(I32)
    carry_ref[...] += n8
    cnt_ref[...] = jnp.broadcast_to(carry_ref[...], cnt_ref.shape).astype(I32)

    rem = sel
    for k in range(TOPK):
        ek = jnp.min(jnp.where(rem, eidx, NE), axis=0, keepdims=True)
        hit = eidx == ek
        slot_ref[k:k + 1, :] = jnp.sum(jnp.where(hit, slot_dense, 0.0), axis=0, keepdims=True).astype(I32)
        w_ref[k:k + 1, :] = jnp.sum(jnp.where(hit, cw, 0.0), axis=0, keepdims=True)
        rem = rem & jnp.logical_not(hit)


def _route(x, g, ada_p, ada_s, wr_hi, wr_lo, rbias):
    scp, scs = _ada_specs(4, TS)
    shp, shs = _ada_specs(3, TS)
    tok_spec = pl.BlockSpec((TOPK, TS), lambda i: (0, i))
    return pl.pallas_call(
        _route_kernel,
        grid=(NTS,),
        in_specs=[
            pl.BlockSpec((TS, D), lambda i: (i, 0)), _const_spec((1, D)), scp, scs, shp, shs,
            _const_spec((NE, D)), _const_spec((NE, D)), _const_spec((NE, 1)),
        ],
        out_specs=[pl.BlockSpec((TS, D), lambda i: (i, 0)), tok_spec, tok_spec,
                   pl.BlockSpec((NE, LANE), lambda i: (i, 0)), _const_spec((NE, LANE))],
        out_shape=[
            jax.ShapeDtypeStruct((R, D), BF16),
            jax.ShapeDtypeStruct((TOPK, R), I32),
            jax.ShapeDtypeStruct((TOPK, R), F32),
            jax.ShapeDtypeStruct((NTS * NE, LANE), I32),
            jax.ShapeDtypeStruct((NE, LANE), I32),
        ],
        scratch_shapes=[pltpu.VMEM((TS, D), F32), pltpu.VMEM((NE, 1), F32)],
        compiler_params=_cp("arbitrary"),
        name="route",
    )(x, g, ada_p, ada_s, ada_p, ada_s, wr_hi, wr_lo, rbias)


U32 = jnp.uint32
HI_MASK = 0xFFFF0000
SEG_CHUNK = 256
COMB_CHUNK = 512


def _pack_pair(lo, hi):
    lo_bits = lax.shift_right_logical(lax.bitcast_convert_type(lo, U32), jnp.uint32(16))
    hi_bits = lax.bitcast_convert_type(hi, U32) & jnp.uint32(HI_MASK)
    return hi_bits | lo_bits


def _unpack_pair(words):
    lo = lax.bitcast_convert_type(lax.shift_left(words, jnp.uint32(16)), F32)
    hi = lax.bitcast_convert_type(words & jnp.uint32(HI_MASK), F32)
    return lo.astype(BF16), hi.astype(BF16)


def _round_bf16(x):
    return x.astype(BF16).astype(F32)


def _aligned(start, size):
    hint = lambda v: v if isinstance(v, int) else pl.multiple_of(v, SUBLANE)
    return pl.ds(hint(start), hint(size))


def _segment_copies(tile, n8_ref, seg_ref, toff_ref, base_ref, make_copy):
    def per_expert(e, carry):
        idx = tile * NE + e
        n8 = n8_ref[idx]
        tile_row = seg_ref[idx]
        global_row = base_ref[e] + toff_ref[idx]

        @pl.when(n8 > 0)
        def _():
            make_copy(tile_row, global_row, n8).start()

        return carry

    lax.fori_loop(0, NE, per_expert, 0)


def _tile_rows(tile, n8_ref, seg_ref):
    last = tile * NE + NE - 1
    return seg_ref[last] + n8_ref[last]


def _dispatch_kernel(n8_ref, seg_ref, toff_ref, base_ref, cnt_ref, h_ref, slot_ref, xb_ref,
                     s_ref, z_ref, sem, zsem):
    tile = pl.program_id(0)
    buf = tile % 2

    def wait_tile(t):
        rows = _tile_rows(t, n8_ref, seg_ref)
        pltpu.make_async_copy(s_ref.at[t % 2, _aligned(0, rows), :], xb_ref.at[_aligned(0, rows), :],
                              sem.at[t % 2]).wait()

    @pl.when(tile == 0)
    def _():
        z_ref[...] = jnp.zeros_like(z_ref)

        def pad_copy(e):
            n = cnt_ref[e]
            npad = (BLK - n % BLK) % BLK
            return npad, pltpu.make_async_copy(z_ref.at[_aligned(0, npad), :],
                                               xb_ref.at[_aligned(base_ref[e] + n, npad), :], zsem)

        def start(e, carry):
            npad, cp = pad_copy(e)

            @pl.when(npad > 0)
            def _():
                cp.start()

            return carry

        def wait(e, carry):
            npad, cp = pad_copy(e)

            @pl.when(npad > 0)
            def _():
                cp.wait()

            return carry

        lax.fori_loop(0, NE, start, 0)
        lax.fori_loop(0, NE, wait, 0)

    @pl.when(tile >= 2)
    def _():
        wait_tile(tile - 2)

    h = h_ref[...]
    for c in range(SEG // SEG_CHUNK):
        rows = c * SEG_CHUNK + lax.broadcasted_iota(I32, (SEG_CHUNK, TS), 0)
        onehot = jnp.zeros((SEG_CHUNK, TS), F32)
        for k in range(TOPK):
            onehot = jnp.where(rows == slot_ref[k:k + 1, :], 1.0, onehot)
        s = jnp.dot(onehot.astype(BF16), h, preferred_element_type=F32)
        s_ref[buf, c * SEG_CHUNK:(c + 1) * SEG_CHUNK, :] = _pack_pair(s[:, :DH], s[:, DH:])

    def make_copy(tile_row, global_row, size):
        return pltpu.make_async_copy(s_ref.at[buf, _aligned(tile_row, size), :],
                                     xb_ref.at[_aligned(global_row, size), :], sem.at[buf])

    _segment_copies(tile, n8_ref, seg_ref, toff_ref, base_ref, make_copy)

    @pl.when(tile == NTS - 1)
    def _():
        wait_tile(tile - 1)
        wait_tile(tile)


def _dispatch(meta, base8, cnt8, hb, slot):
    n8, seg, toff = meta
    return pl.pallas_call(
        _dispatch_kernel,
        grid_spec=pltpu.PrefetchScalarGridSpec(
            num_scalar_prefetch=5,
            grid=(NTS,),
            in_specs=[pl.BlockSpec((TS, D), lambda i, *_: (i, 0)),
                      pl.BlockSpec((TOPK, TS), lambda i, *_: (0, i))],
            out_specs=pl.BlockSpec(memory_space=pl.ANY),
            scratch_shapes=[pltpu.VMEM((2, SEG, DH), U32), pltpu.VMEM((BLK, DH), U32),
                            pltpu.SemaphoreType.DMA((2,)), pltpu.SemaphoreType.DMA],
        ),
        out_shape=jax.ShapeDtypeStruct((NROWS, DH), U32),
        compiler_params=_cp("arbitrary"),
        name="dispatch",
    )(n8, seg, toff, base8, cnt8, hb, slot)


def _experts_kernel(layer, be_ref, nx_ref, nu_ref, xb_ref, w1_hbm, w3_hbm, w2_hbm, yb_ref,
                    w1s_ref, w3s_ref, w2s_ref, w1b_ref, w3b_ref, w2b_ref, sem):
    b = pl.program_id(0)
    live = b < nu_ref[0]
    e = be_ref[b]
    fresh = (b == 0) | (e != be_ref[jnp.maximum(b - 1, 0)])

    def fetch(expert):
        return [pltpu.make_async_copy(hbm.at[layer, expert], stage, sem.at[j])
                for j, (hbm, stage) in enumerate(((w1_hbm, w1s_ref), (w3_hbm, w3s_ref), (w2_hbm, w2s_ref)))]

    @pl.when(b == 0)
    def _():
        for cp in fetch(e):
            cp.start()

    @pl.when(live & fresh)
    def _():
        for cp in fetch(e):
            cp.wait()
        w1b_ref[...] = w1s_ref[...].astype(BF16)
        w3b_ref[...] = w3s_ref[...].astype(BF16)
        w2b_ref[...] = w2s_ref[...].astype(BF16)
        nxt = nx_ref[b]

        @pl.when(nxt != e)
        def _():
            for cp in fetch(nxt):
                cp.start()

    @pl.when(live)
    def _():
        x_lo, x_hi = _unpack_pair(xb_ref[...])

        def up(w_ref):
            return (jnp.dot(x_lo, w_ref[:DH, :], preferred_element_type=F32)
                    + jnp.dot(x_hi, w_ref[DH:, :], preferred_element_type=F32))

        hid = (_silu(up(w1b_ref)) * up(w3b_ref)).astype(BF16)
        for c in range(DH // MXU_N):
            cols = slice(c * MXU_N, (c + 1) * MXU_N)
            y_lo = jnp.dot(hid, w2b_ref[:, c * MXU_N:(c + 1) * MXU_N], preferred_element_type=F32)
            y_hi = jnp.dot(hid, w2b_ref[:, DH + c * MXU_N:DH + (c + 1) * MXU_N], preferred_element_type=F32)
            yb_ref[:, cols] = _pack_pair(_round_bf16(y_lo), _round_bf16(y_hi))


def _experts(blk_e, next_e, n_used, xb, w1, w3, w2, layer):
    row_map = lambda b, be, nx, nu: (jnp.minimum(b, nu[0] - 1), 0)
    hbm = pl.BlockSpec(memory_space=pl.ANY)
    return pl.pallas_call(
        functools.partial(_experts_kernel, layer),
        grid_spec=pltpu.PrefetchScalarGridSpec(
            num_scalar_prefetch=3,
            grid=(NB,),
            in_specs=[pl.BlockSpec((BLK, DH), row_map), hbm, hbm, hbm],
            out_specs=pl.BlockSpec((BLK, DH), row_map),
            scratch_shapes=[pltpu.VMEM((D, DE), F32), pltpu.VMEM((D, DE), F32), pltpu.VMEM((DE, D), F32),
                            pltpu.VMEM((D, DE), BF16), pltpu.VMEM((D, DE), BF16), pltpu.VMEM((DE, D), BF16),
                            pltpu.SemaphoreType.DMA((3,))],
        ),
        out_shape=jax.ShapeDtypeStruct((NROWS, DH), U32),
        compiler_params=_cp("arbitrary"),
        name="experts",
    )(blk_e, next_e, n_used, xb, w1, w3, w2)


def _shared_kernel(h_ref, w1_ref, w3_ref, w2_ref, o_ref):
    x = h_ref[...].astype(BF16)
    hid = (_silu(jnp.dot(x, w1_ref[...], preferred_element_type=F32))
           * jnp.dot(x, w3_ref[...], preferred_element_type=F32))
    o_ref[...] = jnp.dot(hid.astype(BF16), w2_ref[...], preferred_element_type=F32)


def _shared(h2, ws1, ws3, ws2):
    return pl.pallas_call(
        _shared_kernel,
        grid=(NT,),
        in_specs=[pl.BlockSpec((TM, D), lambda i: (i, 0)),
                  _const_spec((D, DE)), _const_spec((D, DE)), _const_spec((DE, D))],
        out_specs=pl.BlockSpec((TM, D), lambda i: (i, 0)),
        out_shape=jax.ShapeDtypeStruct((R, D), F32),
        compiler_params=_cp("arbitrary"),
        name="shared",
    )(h2, ws1, ws3, ws2)


def _combine_kernel(n8_ref, seg_ref, toff_ref, base_ref, x_ref, sh_ref, slot_ref, wt_ref, gp_ref, gs_ref,
                    gpost_ref, yb_ref, o_ref, ys_ref, f_ref, sem):
    tile = pl.program_id(0)
    buf = tile % 2

    def fetch_tile(t):
        def make_copy(tile_row, global_row, size):
            return pltpu.make_async_copy(yb_ref.at[_aligned(global_row, size), :],
                                         ys_ref.at[t % 2, _aligned(tile_row, size), :], sem.at[t % 2])

        _segment_copies(t, n8_ref, seg_ref, toff_ref, base_ref, make_copy)

    @pl.when(tile == 0)
    def _():
        ys_ref[...] = jnp.zeros_like(ys_ref)
        fetch_tile(tile)

    @pl.when(tile + 1 < NTS)
    def _():
        fetch_tile(tile + 1)

    cols = lax.broadcasted_iota(I32, (TS, SEG), 1)
    cmat = jnp.zeros((TS, SEG), F32)
    for k in range(TOPK):
        cmat = jnp.where(cols == slot_ref[:, k:k + 1], wt_ref[:, k:k + 1], cmat)
    cmat = cmat.astype(BF16)

    rows = _tile_rows(tile, n8_ref, seg_ref)
    pltpu.make_async_copy(yb_ref.at[_aligned(0, rows), :], ys_ref.at[buf, _aligned(0, rows), :],
                          sem.at[buf]).wait()

    f_ref[...] = sh_ref[...]
    for c in range(SEG // COMB_CHUNK):
        ks = slice(c * COMB_CHUNK, (c + 1) * COMB_CHUNK)
        y_lo, y_hi = _unpack_pair(ys_ref[buf, ks, :])
        f_ref[:, :DH] += jnp.dot(cmat[:, ks], y_lo, preferred_element_type=F32)
        f_ref[:, DH:] += jnp.dot(cmat[:, ks], y_hi, preferred_element_type=F32)
    gate = jnp.where(tile < RP // TS, gp_ref[...], gs_ref[...])
    gpost = gpost_ref[...]
    for s in range(TS // SUB):
        rows = slice(s * SUB, (s + 1) * SUB)
        o_ref[rows, :] = x_ref[rows, :] + gate * (_rms(f_ref[rows, :]) * gpost)


def _combine(meta, base8, x, shared, slot_t, w_t, ada_p, ada_s, gpost, yb):
    n8, seg, toff = meta
    tiles_per_seq = LP // TS
    gp = pl.BlockSpec((None, 1, D), lambda i, *_: (jnp.minimum(i // tiles_per_seq, NP - 1), 0, 5))
    gs = pl.BlockSpec((NS, D), lambda i, *_: (0, 5))
    row = pl.BlockSpec((TS, D), lambda i, *_: (i, 0))
    tok = pl.BlockSpec((TS, TOPK), lambda i, *_: (i, 0))
    return pl.pallas_call(
        _combine_kernel,
        grid_spec=pltpu.PrefetchScalarGridSpec(
            num_scalar_prefetch=4,
            grid=(NTS,),
            in_specs=[row, row, tok, tok, gp, gs, pl.BlockSpec((1, D), lambda i, *_: (0, 0)),
                      pl.BlockSpec(memory_space=pl.ANY)],
            out_specs=row,
            scratch_shapes=[pltpu.VMEM((2, SEG, DH), U32), pltpu.VMEM((TS, D), F32),
                            pltpu.SemaphoreType.DMA((2,))],
        ),
        out_shape=jax.ShapeDtypeStruct((R, D), F32),
        compiler_params=_cp("arbitrary"),
        name="combine",
    )(n8, seg, toff, base8, x, shared, slot_t, w_t, ada_p, ada_s, gpost, yb)


def _row(v):
    return v.reshape(1, -1)


def _to_rows(x_prompt, x_sample):
    return jnp.concatenate([x_prompt.reshape(RP, D), x_sample.transpose(1, 0, 2).reshape(RS, D)], axis=0)


def _sample_rows_to_batch(rows, width):
    return rows.reshape(LS, NS, width).transpose(1, 0, 2)


def kernel(x_prompt, x_sample, state_conv, state_pool, c_prompt, c_sample, w_ada, b_ada, g_pre_mix,
           g_post_mix, g_pre_ffn, g_post_ffn, w_in, conv_w, conv_b, conv_ln_g, conv_ln_b, conv_w_pw,
           gmlp_ln_g, gmlp_ln_b, gmlp_w_s, gmlp_b_s, gmlp_w_o, pool_w, pool_scale, w_out, router_w,
           router_bias, w1, w3, w2, ws1, ws3, ws2):
    x = _to_rows(x_prompt, x_sample)
    ada = _ada(jnp.concatenate([c_prompt, c_sample], axis=0), w_ada, b_ada)

    conv_p, pool_p, conv_s, pool_s, v_s = [], [], [], [], []
    for l in range(DEPTH):
        ada_p = ada[l, :NP].reshape(NP, 1, 6 * D)
        ada_s = ada[l, NP:]

        h = _hmod(x, _row(g_pre_mix[l]), ada_p, ada_s, 1, 0)
        glu = _colmm(h, w_in, l, (0, DC), DC, 512, _glu_epilogue, F32, "in_glu")
        zb = _colmm(h, w_in, l, (COL_GMLP,), 2 * DC, 1024, _gelu_epilogue, F32, "in_gelu")
        p = _colmm(h, w_in, l, (COL_POOL,), DC, 1024, _id_epilogue, F32, "in_pool")
        gates = _colmm(h, w_in, l, (COL_GATE,), 3 * D, 1024, _sigmoid_epilogue, BF16, "in_gate")

        cw, cb = conv_w[l], _row(conv_b[l])
        clg, clb = _row(conv_ln_g[l]), _row(conv_ln_b[l])
        a_p = _conv_p(glu, cw, cb, clg, clb)
        a_s = _conv_s(glu, state_conv[l].transpose(1, 0, 2), cw, cb, clg, clb)

        glg, glb = _row(gmlp_ln_g[l]), _row(gmlp_ln_b[l])
        b_p = _gmlp_p(zb, gmlp_w_s[l], gmlp_b_s[l].T, glg, glb)
        coef = jnp.repeat(gmlp_w_s[l][:, :LS, :LS].transpose(1, 2, 0).reshape(LS * LS, HEADS), DC // HEADS, axis=1)
        bias = jnp.repeat(gmlp_b_s[l][:, :LS].T, DC // HEADS, axis=1)
        b_s, v_rows = _gmlp_s(zb, coef, bias, glg, glb)

        c_p = _pool_p(p)
        c_s = _pool_s(p, state_pool[l].transpose(1, 0, 2))

        x = _merge(((a_p, a_s), (b_p, b_s), (c_p, c_s)), gates, x, ada_p, ada_s, _row(g_post_mix[l]),
                   conv_w_pw[l].astype(BF16), gmlp_w_o[l].astype(BF16), pool_w[l].astype(BF16),
                   _row(pool_scale[l]), w_out[l].astype(BF16))

        glu_p = glu[:RP].reshape(NP, LP, DC)
        p_p = p[:RP].reshape(NP, LP, DC)
        conv_p.append(glu_p[:, LP - HIST_C:])
        pool_p.append(p_p[:, LP - HIST_P:])
        conv_s.append(jnp.concatenate([state_conv[l][:, LS:], _sample_rows_to_batch(glu[RP:], DC)], axis=1))
        pool_s.append(jnp.concatenate([state_pool[l][:, LS:], _sample_rows_to_batch(p[RP:], DC)], axis=1))
        v_s.append(_sample_rows_to_batch(v_rows, DC))

        wr_t = router_w[l].T
        wr_hi = wr_t.astype(BF16)
        wr_lo = (wr_t - wr_hi.astype(F32)).astype(BF16)
        hb, slot, w_k, meta, cnt = _route(x, _row(g_pre_ffn[l]), ada_p, ada_s, wr_hi, wr_lo,
                                          router_bias[l].reshape(NE, 1))
        meta = tuple(meta[:, c] for c in range(3))
        cnt8 = cnt[:, 0]
        padded = (cnt8 + BLK - 1) // BLK * BLK
        pad_end = jnp.cumsum(padded)
        base8 = (pad_end - padded).astype(I32)
        n_used = (pad_end[-1] // BLK).astype(I32)
        blk = jnp.minimum(jnp.arange(NB, dtype=I32), n_used - 1)
        blk_e = jnp.minimum(jnp.sum((pad_end[None, :] <= (blk * BLK)[:, None]).astype(I32), axis=1), NE - 1)
        next_blk = jnp.minimum(pad_end[blk_e] // BLK, n_used - 1)
        next_e = blk_e[next_blk]

        xb = _dispatch(meta, base8, cnt8, hb, slot)
        yb = _experts(blk_e, next_e, n_used.reshape(1), xb, w1, w3, w2, l)
        shared = _shared(hb, ws1[l].astype(BF16), ws3[l].astype(BF16), ws2[l].astype(BF16))
        x = _combine(meta, base8, x, shared, slot.T, w_k.T, ada_p, ada_s, _row(g_post_ffn[l]), yb)

    y_prompt = x[:RP].reshape(NP, LP, D)
    y_sample = _sample_rows_to_batch(x[RP:], D)
    return (y_prompt, y_sample, jnp.stack(conv_p), jnp.stack(pool_p), jnp.stack(conv_s),
            jnp.stack(pool_s), jnp.stack(v_s))
```

```python
import functools

import jax
import jax.numpy as jnp
from jax import lax
from jax.experimental import pallas as pl
from jax.experimental.pallas import tpu as pltpu

F32 = jnp.float32
BF16 = jnp.bfloat16
I32 = jnp.int32

D = 2048
NP, LP = 4, 2048
NS, LS = 128, 4
RP = NP * LP
RS = NS * LS
R = RP + RS
DEPTH = 2
DC = 1024
CONV_W = 31
HIST_C = CONV_W - 1
HIST_P = 15
POOL_WINDOWS = (2, 4, 8, 16)
HEADS = 8
CHUNK = 128
D_IN = 5 * DC + 3 * D
COL_GMLP, COL_POOL, COL_GATE = 2 * DC, 4 * DC, 5 * DC
NE, TOPK, NGRP, TOPG = 64, 8, 8, 4
DE = 512
ROUTE_SCALE = 2.5
EPS = 1e-6

TM = 512
NT = R // TM
NTP = RP // TM
SUB = 128
NSUB = TM // SUB
TMM = 256
LANE = 128
SUBLANE = 8
MXU_N = 256
TS = 256
NTS = R // TS
SEG = TOPK * TS + NE * SUBLANE
DH = D // 2
BLK = 256
NB = -(-(R * TOPK + NTS * NE * (SUBLANE - 1) + NE * (BLK - 1)) // BLK)
NROWS = NB * BLK

VMEM_LIMIT = 56 * 1024 * 1024


def _cp(*sem):
    return pltpu.CompilerParams(dimension_semantics=sem, vmem_limit_bytes=VMEM_LIMIT)


def _const_spec(shape):
    nd = len(shape)
    return pl.BlockSpec(shape, lambda *_: (0,) * nd, pipeline_mode=pl.Buffered(1))


def _layer_spec(shape, layer):
    nd = len(shape)
    return pl.BlockSpec((None,) + tuple(shape), lambda *_: (layer,) + (0,) * nd, pipeline_mode=pl.Buffered(1))


def _rms(x):
    return x * lax.rsqrt(jnp.mean(x * x, axis=-1, keepdims=True) + EPS)


def _ln(x, g, b):
    mu = jnp.mean(x, axis=-1, keepdims=True)
    xc = x - mu
    var = jnp.mean(xc * xc, axis=-1, keepdims=True)
    return xc * lax.rsqrt(var + EPS) * g + b


def _silu(x):
    return x * jax.nn.sigmoid(x)


ADA_TN = 1024


def _ada_kernel(c_ref, w_ref, b_ref, o_ref):
    c = _silu(c_ref[...]).astype(BF16)
    o_ref[...] = jnp.dot(c, w_ref[...].astype(BF16), preferred_element_type=F32) + b_ref[...]


def _ada(c_all, w_ada, b_ada):
    n = c_all.shape[0]
    return pl.pallas_call(
        _ada_kernel,
        grid=(DEPTH, 6 * D // ADA_TN),
        in_specs=[
            pl.BlockSpec((n, D), lambda l, j: (0, 0)),
            pl.BlockSpec((None, D, ADA_TN), lambda l, j: (l, 0, j)),
            pl.BlockSpec((None, 1, ADA_TN), lambda l, j: (l, 0, j)),
        ],
        out_specs=pl.BlockSpec((None, n, ADA_TN), lambda l, j: (l, 0, j)),
        out_shape=jax.ShapeDtypeStruct((DEPTH, n, 6 * D), F32),
        compiler_params=_cp("arbitrary", "arbitrary"),
        name="ada",
    )(c_all, w_ada, b_ada.reshape(DEPTH, 1, 6 * D))


def _ada_specs(chunk, tile_rows):
    tiles_per_seq = LP // tile_rows
    p = pl.BlockSpec((None, 1, D), lambda i: (jnp.minimum(i // tiles_per_seq, NP - 1), 0, chunk))
    s = pl.BlockSpec((NS, D), lambda i: (0, chunk))
    return p, s


def _hmod_kernel(x_ref, g_ref, scp_ref, scs_ref, shp_ref, shs_ref, o_ref):
    is_p = pl.program_id(0) < NTP
    sc = 1.0 + jnp.where(is_p, scp_ref[...], scs_ref[...])
    sh = jnp.where(is_p, shp_ref[...], shs_ref[...])
    g = g_ref[...]
    for s in range(NSUB):
        rows = slice(s * SUB, (s + 1) * SUB)
        o_ref[rows, :] = (_rms(x_ref[rows, :]) * g * sc + sh).astype(o_ref.dtype)


def _hmod(x, g, ada_p, ada_s, c_scale, c_shift):
    scp, scs = _ada_specs(c_scale, TM)
    shp, shs = _ada_specs(c_shift, TM)
    return pl.pallas_call(
        _hmod_kernel,
        grid=(NT,),
        in_specs=[pl.BlockSpec((TM, D), lambda i: (i, 0)), _const_spec((1, D)), scp, scs, shp, shs],
        out_specs=pl.BlockSpec((TM, D), lambda i: (i, 0)),
        out_shape=jax.ShapeDtypeStruct((R, D), BF16),
        compiler_params=_cp("arbitrary"),
        name="hmod",
    )(x, g, ada_p, ada_s, ada_p, ada_s)


def _colmm_kernel(nw, epilogue, *refs):
    lhs_ref = refs[0]
    w_refs = refs[1:1 + nw]
    o_ref = refs[1 + nw]
    wb_refs = refs[2 + nw:]

    @pl.when(pl.program_id(1) == 0)
    def _():
        for w_ref, wb_ref in zip(w_refs, wb_refs):
            wb_ref[...] = w_ref[...].astype(BF16)

    lhs = lhs_ref[...]
    accs = [jnp.dot(lhs, wb_ref[...], preferred_element_type=F32) for wb_ref in wb_refs]
    o_ref[...] = epilogue(*accs).astype(o_ref.dtype)


def _colmm(h, w_in, layer, col_offsets, n_cols, tn, epilogue, out_dtype, name):
    nw = len(col_offsets)
    w_specs = [
        pl.BlockSpec((None, D, tn), functools.partial(lambda j, i, o: (layer, 0, o + j), o=off // tn))
        for off in col_offsets
    ]
    return pl.pallas_call(
        functools.partial(_colmm_kernel, nw, epilogue),
        grid=(n_cols // tn, NT),
        in_specs=[pl.BlockSpec((TM, D), lambda j, i: (i, 0))] + w_specs,
        out_specs=pl.BlockSpec((TM, tn), lambda j, i: (i, j)),
        out_shape=jax.ShapeDtypeStruct((R, n_cols), out_dtype),
        scratch_shapes=[pltpu.VMEM((D, tn), BF16) for _ in range(nw)],
        compiler_params=_cp("arbitrary", "arbitrary"),
        name=name,
    )(h, *([w_in] * nw))


def _glu_epilogue(a, g):
    return a * jax.nn.sigmoid(g)


def _gelu_epilogue(z):
    return 0.5 * z * (1.0 + lax.erf(z * (0.5 ** 0.5)))


def _id_epilogue(z):
    return z


def _sigmoid_epilogue(z):
    return jax.nn.sigmoid(z)


CONV_HALO = 32


def _conv_taps(w_ref, slab, cs):
    acc = None
    for k in range(CONV_W):
        term = w_ref[k:k + 1, cs] * slab(k)
        acc = term if acc is None else acc + term
    return acc


def _conv_p_kernel(main_ref, halo_ref, w_ref, b_ref, lg_ref, lb_ref, o_ref, xs_ref, y_ref):
    first = (pl.program_id(0) % (LP // SUB)) == 0
    xs_ref[0, 0:CONV_HALO, :] = jnp.where(first, 0.0, halo_ref[...])
    xs_ref[0, CONV_HALO:, :] = main_ref[...]
    shift_rows = CONV_HALO + SUB - SUBLANE
    for s in range(1, SUBLANE):
        xs_ref[s, 0:shift_rows, :] = xs_ref[0, s:s + shift_rows, :]
    base = CONV_HALO - HIST_C

    def window(k, cs):
        s = (base + k) % SUBLANE
        start = base + k - s
        return xs_ref[s, start:start + SUB, cs]

    def lane_chunk(c, carry):
        cs = pl.ds(pl.multiple_of(c * LANE, LANE), LANE)
        acc = _conv_taps(w_ref, functools.partial(window, cs=cs), cs)
        y_ref[:, cs] = acc + b_ref[:, cs]
        return carry

    lax.fori_loop(0, DC // LANE, lane_chunk, 0)
    for r in range(SUB // LN_ROWS):
        rs = slice(r * LN_ROWS, (r + 1) * LN_ROWS)
        o_ref[rs, :] = _silu(_ln(y_ref[rs, :], lg_ref[...], lb_ref[...])).astype(o_ref.dtype)


def _conv_p(glu, w, b, lg, lb):
    halo_per_tile = SUB // CONV_HALO
    return pl.pallas_call(
        _conv_p_kernel,
        grid=(RP // SUB,),
        in_specs=[
            pl.BlockSpec((SUB, DC), lambda i: (i, 0)),
            pl.BlockSpec((CONV_HALO, DC), lambda i: (jnp.maximum(i * halo_per_tile - 1, 0), 0)),
            _const_spec((CONV_W, DC)), _const_spec((1, DC)), _const_spec((1, DC)), _const_spec((1, DC)),
        ],
        out_specs=pl.BlockSpec((SUB, DC), lambda i: (i, 0)),
        out_shape=jax.ShapeDtypeStruct((RP, DC), BF16),
        scratch_shapes=[pltpu.VMEM((SUBLANE, CONV_HALO + SUB, DC), F32), pltpu.VMEM((SUB, DC), F32)],
        compiler_params=_cp("arbitrary"),
        name="conv_prompt",
    )(glu, glu, w, b, lg, lb)


def _conv_s_kernel(glu_ref, st_ref, w_ref, b_ref, lg_ref, lb_ref, o_ref, y_ref):
    for l in range(LS):
        rows = slice(l * SUB, (l + 1) * SUB)

        def slab(k, cs):
            j = l + k
            if j < HIST_C:
                return st_ref[j, :, cs]
            return glu_ref[(j - HIST_C) * SUB:(j - HIST_C + 1) * SUB, cs]

        for c in range(DC // LANE):
            cs = slice(c * LANE, (c + 1) * LANE)
            acc = _conv_taps(w_ref, functools.partial(slab, cs=cs), cs)
            y_ref[:, cs] = acc + b_ref[:, cs]
        o_ref[rows, :] = _silu(_ln(y_ref[...], lg_ref[...], lb_ref[...])).astype(o_ref.dtype)


def _conv_s(glu, state_t, w, b, lg, lb):
    return pl.pallas_call(
        _conv_s_kernel,
        grid=(1,),
        in_specs=[
            pl.BlockSpec((RS, DC), lambda i: (RP // RS, 0)),
            _const_spec((HIST_C, NS, DC)),
            _const_spec((CONV_W, DC)), _const_spec((1, DC)), _const_spec((1, DC)), _const_spec((1, DC)),
        ],
        out_specs=_const_spec((RS, DC)),
        out_shape=jax.ShapeDtypeStruct((RS, DC), BF16),
        scratch_shapes=[pltpu.VMEM((SUB, DC), F32)],
        compiler_params=_cp("arbitrary"),
        name="conv_sample",
    )(glu, state_t, w, b, lg, lb)


def _gmlp_p_kernel(u_ref, v_ref, ws_ref, bst_ref, lg_ref, lb_ref, o_ref, wsb_ref):
    tgt = lax.broadcasted_iota(I32, (CHUNK, CHUNK), 0)
    src = lax.broadcasted_iota(I32, (CHUNK, CHUNK), 1)
    causal = src <= tgt
    hd = DC // HEADS
    for h in range(HEADS):
        wsb_ref[h] = jnp.where(causal, ws_ref[h], 0.0).astype(BF16)
    for n in range(TM // CHUNK):
        rows = slice(n * CHUNK, (n + 1) * CHUNK)
        v = _ln(v_ref[rows, :], lg_ref[...], lb_ref[...]).astype(BF16)
        for h in range(HEADS):
            cs = slice(h * hd, (h + 1) * hd)
            mixed = jnp.dot(wsb_ref[h], v[:, cs], preferred_element_type=F32) + bst_ref[:, h:h + 1]
            o_ref[rows, cs] = (u_ref[rows, cs] * mixed).astype(o_ref.dtype)


def _gmlp_p(zb, ws, bst, lg, lb):
    return pl.pallas_call(
        _gmlp_p_kernel,
        grid=(RP // TM,),
        in_specs=[
            pl.BlockSpec((TM, DC), lambda i: (i, 0)),
            pl.BlockSpec((TM, DC), lambda i: (i, 1)),
            _const_spec((HEADS, CHUNK, CHUNK)), _const_spec((CHUNK, HEADS)),
            _const_spec((1, DC)), _const_spec((1, DC)),
        ],
        out_specs=pl.BlockSpec((TM, DC), lambda i: (i, 0)),
        scratch_shapes=[pltpu.VMEM((HEADS, CHUNK, CHUNK), BF16)],
        out_shape=jax.ShapeDtypeStruct((RP, DC), BF16),
        compiler_params=_cp("arbitrary"),
        name="gmlp_prompt",
    )(zb, zb, ws, bst, lg, lb)


def _gmlp_s_kernel(u_ref, v_ref, coef_ref, bias_ref, lg_ref, lb_ref, o_ref, vo_ref):
    for l in range(LS):
        rows = slice(l * SUB, (l + 1) * SUB)
        vo_ref[rows, :] = _ln(v_ref[rows, :], lg_ref[...], lb_ref[...])
    for l in range(LS):
        rows = slice(l * SUB, (l + 1) * SUB)
        mixed = bias_ref[l:l + 1, :]
        for s in range(l + 1):
            mixed = mixed + coef_ref[l * LS + s:l * LS + s + 1, :] * vo_ref[s * SUB:(s + 1) * SUB, :]
        o_ref[rows, :] = (u_ref[rows, :] * mixed).astype(o_ref.dtype)


def _gmlp_s(zb, coef, bias, lg, lb):
    return pl.pallas_call(
        _gmlp_s_kernel,
        grid=(1,),
        in_specs=[
            pl.BlockSpec((RS, DC), lambda i: (RP // RS, 0)),
            pl.BlockSpec((RS, DC), lambda i: (RP // RS, 1)),
            _const_spec((LS * LS, DC)), _const_spec((LS, DC)), _const_spec((1, DC)), _const_spec((1, DC)),
        ],
        out_specs=[_const_spec((RS, DC)), _const_spec((RS, DC))],
        out_shape=[jax.ShapeDtypeStruct((RS, DC), BF16), jax.ShapeDtypeStruct((RS, DC), F32)],
        compiler_params=_cp("arbitrary"),
        name="gmlp_sample",
    )(zb, zb, coef, bias, lg, lb)


POOL_HALO = 16
GD = DC // len(POOL_WINDOWS)


def _pool_p_kernel(main_ref, halo_ref, o_ref, xc_ref):
    t_tile = pl.program_id(0) % (LP // TM)
    xc_ref[0:POOL_HALO, :] = jnp.where(t_tile == 0, 0.0, halo_ref[...])
    xc_ref[POOL_HALO:, :] = main_ref[...]
    for n in range(TM // SUB):
        first = POOL_HALO + n * SUB
        pos = t_tile * TM + n * SUB + lax.broadcasted_iota(I32, (SUB, 1), 0)
        for g, w in enumerate(POOL_WINDOWS):
            cs = slice(g * GD, (g + 1) * GD)
            tok = xc_ref[first:first + SUB, cs]
            acc = tok
            for j in range(1, w):
                acc = acc + xc_ref[first - j:first - j + SUB, cs]
            cnt = jnp.minimum(pos + 1, w).astype(F32)
            o_ref[n * SUB:(n + 1) * SUB, cs] = (acc / cnt - tok).astype(o_ref.dtype)


def _pool_p(p):
    halo_per_tile = TM // POOL_HALO
    return pl.pallas_call(
        _pool_p_kernel,
        grid=(RP // TM,),
        in_specs=[
            pl.BlockSpec((TM, DC), lambda i: (i, 0)),
            pl.BlockSpec((POOL_HALO, DC), lambda i: (jnp.maximum(i * halo_per_tile - 1, 0), 0)),
        ],
        out_specs=pl.BlockSpec((TM, DC), lambda i: (i, 0)),
        out_shape=jax.ShapeDtypeStruct((RP, DC), BF16),
        scratch_shapes=[pltpu.VMEM((POOL_HALO + TM, DC), F32)],
        compiler_params=_cp("arbitrary"),
        name="pool_prompt",
    )(p, p)


def _pool_s_kernel(p_ref, st_ref, o_ref):
    def slab(j, cs):
        if j < HIST_P:
            return st_ref[j, :, cs]
        return p_ref[(j - HIST_P) * SUB:(j - HIST_P + 1) * SUB, cs]

    for l in range(LS):
        for g, w in enumerate(POOL_WINDOWS):
            cs = slice(g * GD, (g + 1) * GD)
            tok = slab(HIST_P + l, cs)
            acc = tok
            for j in range(1, w):
                acc = acc + slab(HIST_P + l - j, cs)
            o_ref[l * SUB:(l + 1) * SUB, cs] = (acc / float(w) - tok).astype(o_ref.dtype)


def _pool_s(p, state_t):
    return pl.pallas_call(
        _pool_s_kernel,
        grid=(1,),
        in_specs=[pl.BlockSpec((RS, DC), lambda i: (RP // RS, 0)), _const_spec((HIST_P, NS, DC))],
        out_specs=_const_spec((RS, DC)),
        out_shape=jax.ShapeDtypeStruct((RS, DC), BF16),
        compiler_params=_cp("arbitrary"),
        name="pool_sample",
    )(p, state_t)


def _merge_kernel(ap_ref, as_ref, bp_ref, bs_ref, cp_ref, cs_ref, ga_ref, gb_ref, gc_ref, x_ref,
                  gp_ref, gs_ref, gpost_ref, wpw_ref, wo_ref, wpool_ref, pscale_ref, wout_ref,
                  o_ref, m_ref):
    is_p = pl.program_id(0) < RP // TMM
    a = jnp.where(is_p, ap_ref[...], as_ref[...])
    b = jnp.where(is_p, bp_ref[...], bs_ref[...])
    c = jnp.where(is_p, cp_ref[...], cs_ref[...])
    m_ref[...] = ga_ref[...].astype(F32) * jnp.dot(a, wpw_ref[...], preferred_element_type=F32)
    m_ref[...] += gb_ref[...].astype(F32) * jnp.dot(b, wo_ref[...], preferred_element_type=F32)
    eo = D // len(POOL_WINDOWS)
    for g in range(len(POOL_WINDOWS)):
        os_ = slice(g * eo, (g + 1) * eo)
        yc = jnp.dot(c[:, g * GD:(g + 1) * GD], wpool_ref[g], preferred_element_type=F32)
        m_ref[:, os_] += gc_ref[:, os_].astype(F32) * (yc * pscale_ref[:, os_])
    mix = jnp.dot(m_ref[...].astype(BF16), wout_ref[...], preferred_element_type=F32)
    m_ref[...] = _rms(mix) * gpost_ref[...]
    gate = jnp.where(is_p, gp_ref[...], gs_ref[...])
    for s in range(TMM // SUB):
        rows = slice(s * SUB, (s + 1) * SUB)
        o_ref[rows, :] = x_ref[rows, :] + gate * m_ref[rows, :]


def _merge(acts, gates, x, ada_p, ada_s, gpost, wpw, wo, wpool, pscale, wout, layer):
    (a_p, a_s), (b_p, b_s), (c_p, c_s) = acts
    ntp = RP // TMM
    pspec = pl.BlockSpec((TMM, DC), lambda i: (jnp.minimum(i, ntp - 1), 0))
    sspec = pl.BlockSpec((TMM, DC), lambda i: (jnp.maximum(i - ntp, 0), 0))
    gp, gs = _ada_specs(2, TMM)
    return pl.pallas_call(
        _merge_kernel,
        grid=(R // TMM,),
        in_specs=[
            pspec, sspec, pspec, sspec, pspec, sspec,
            pl.BlockSpec((TMM, D), lambda i: (i, 0)),
            pl.BlockSpec((TMM, D), lambda i: (i, 1)),
            pl.BlockSpec((TMM, D), lambda i: (i, 2)),
            pl.BlockSpec((TMM, D), lambda i: (i, 0)),
            gp, gs, _const_spec((1, D)),
            _layer_spec((DC, D), layer), _layer_spec((DC, D), layer),
            _layer_spec((len(POOL_WINDOWS), GD, D // 4), layer),
            _const_spec((1, D)), _layer_spec((D, D), layer),
        ],
        out_specs=pl.BlockSpec((TMM, D), lambda i: (i, 0)),
        out_shape=jax.ShapeDtypeStruct((R, D), F32),
        scratch_shapes=[pltpu.VMEM((TMM, D), F32)],
        compiler_params=_cp("arbitrary"),
        name="merge",
    )(a_p, a_s, b_p, b_s, c_p, c_s, gates, gates, gates, x, ada_p, ada_s, gpost,
      wpw, wo, wpool, pscale, wout)


def _route_kernel(x_ref, g_ref, scp_ref, scs_ref, shp_ref, shs_ref, wrh_ref, wrl_ref, rb_ref,
                  hb_ref, slot_ref, w_ref, meta_ref, cnt_ref, h_ref, carry_ref):
    TM = TS
    i = pl.program_id(0)

    @pl.when(i == 0)
    def _():
        carry_ref[...] = jnp.zeros_like(carry_ref)

    is_p = i < RP // TS
    sc = 1.0 + jnp.where(is_p, scp_ref[...], scs_ref[...])
    sh = jnp.where(is_p, shp_ref[...], shs_ref[...])
    g = g_ref[...]
    for s in range(TS // SUB):
        rows = slice(s * SUB, (s + 1) * SUB)
        h_ref[rows, :] = _rms(x_ref[rows, :]) * g * sc + sh

    h = h_ref[...]
    h_hi = h.astype(BF16)
    hb_ref[...] = h_hi
    h_lo = (h - h_hi.astype(F32)).astype(BF16)
    nt = (((1,), (1,)), ((), ()))
    logits = (lax.dot_general(wrh_ref[...], h_hi, nt, preferred_element_type=F32)
              + lax.dot_general(wrl_ref[...], h_hi, nt, preferred_element_type=F32)
              + lax.dot_general(wrh_ref[...], h_lo, nt, preferred_element_type=F32))
    scores = jax.nn.sigmoid(logits)
    biased = scores + rb_ref[...]

    ge = NE // NGRP
    g3 = biased.reshape(NGRP, ge, TM)
    idx3 = lax.broadcasted_iota(I32, (NGRP, ge, TM), 1)
    m1 = jnp.max(g3, axis=1, keepdims=True)
    first = jnp.min(jnp.where(g3 == m1, idx3, ge), axis=1, keepdims=True)
    m2 = jnp.max(jnp.where(idx3 == first, -jnp.inf, g3), axis=1, keepdims=True)
    gscore = (m1 + m2).reshape(NGRP, TM)

    gidx = lax.broadcasted_iota(I32, (NGRP, TM), 0)
    grank = jnp.zeros((NGRP, TM), F32)
    for j in range(NGRP):
        sj = gscore[j:j + 1, :]
        grank = grank + jnp.where((sj > gscore) | ((sj == gscore) & (j < gidx)), 1.0, 0.0)
    gkeep = grank < TOPG
    ekeep = jnp.broadcast_to(gkeep.reshape(NGRP, 1, TM), (NGRP, ge, TM)).reshape(NE, TM)
    masked = jnp.where(ekeep, biased, -jnp.inf)

    eidx = lax.broadcasted_iota(I32, (NE, TM), 0)
    erank = jnp.zeros((NE, TM), F32)
    for j in range(NE):
        sj = masked[j:j + 1, :]
        erank = erank + jnp.where((sj > masked) | ((sj == masked) & (j < eidx)), 1.0, 0.0)
    sel = erank < TOPK

    sw = jnp.where(sel, scores, 0.0)
    cw = sw / jnp.sum(sw, axis=0, keepdims=True) * ROUTE_SCALE

    self_ = jnp.where(sel, 1.0, 0.0)
    t_src = lax.broadcasted_iota(I32, (TS, TS), 0)
    t_dst = lax.broadcasted_iota(I32, (TS, TS), 1)
    before = jnp.where(t_src < t_dst, 1.0, 0.0).astype(BF16)
    rank = jnp.dot(self_.astype(BF16), before, preferred_element_type=F32)
    n8 = jnp.floor((jnp.sum(self_, axis=1, keepdims=True) + (SUBLANE - 1)) / SUBLANE) * SUBLANE
    n8b = jnp.broadcast_to(n8, (NE, LANE))
    e_src = lax.broadcasted_iota(I32, (NE, NE), 1)
    e_dst = lax.broadcasted_iota(I32, (NE, NE), 0)
    lower = jnp.where(e_src < e_dst, 1.0, 0.0).astype(BF16)
    segoff = jnp.dot(lower, n8b.astype(BF16), preferred_element_type=F32)
    slot_dense = segoff[:, 0:1] + rank

    lane = lax.broadcasted_iota(I32, (NE, LANE), 1)
    tileoff = jnp.broadcast_to(carry_ref[...], (NE, LANE))
    meta_ref[...] = jnp.where(lane == 0, n8b, jnp.where(lane == 1, segoff, tileoff)).astype(I32)
    carry_ref[...] += n8
    cnt_ref[...] = jnp.broadcast_to(carry_ref[...], cnt_ref.shape).astype(I32)

    rem = sel
    for k in range(TOPK):
        ek = jnp.min(jnp.where(rem, eidx, NE), axis=0, keepdims=True)
        hit = eidx == ek
        slot_ref[k:k + 1, :] = jnp.sum(jnp.where(hit, slot_dense, 0.0), axis=0, keepdims=True).astype(I32)
        w_ref[k:k + 1, :] = jnp.sum(jnp.where(hit, cw, 0.0), axis=0, keepdims=True)
        rem = rem & jnp.logical_not(hit)


def _route(x, g, ada_p, ada_s, wr_hi, wr_lo, rbias):
    scp, scs = _ada_specs(4, TS)
    shp, shs = _ada_specs(3, TS)
    tok_spec = pl.BlockSpec((TOPK, TS), lambda i: (0, i))
    return pl.pallas_call(
        _route_kernel,
        grid=(NTS,),
        in_specs=[
            pl.BlockSpec((TS, D), lambda i: (i, 0)), _const_spec((1, D)), scp, scs, shp, shs,
            _const_spec((NE, D)), _const_spec((NE, D)), _const_spec((NE, 1)),
        ],
        out_specs=[pl.BlockSpec((TS, D), lambda i: (i, 0)), tok_spec, tok_spec,
                   pl.BlockSpec((NE, LANE), lambda i: (i, 0)), _const_spec((NE, LANE))],
        out_shape=[
            jax.ShapeDtypeStruct((R, D), BF16),
            jax.ShapeDtypeStruct((TOPK, R), I32),
            jax.ShapeDtypeStruct((TOPK, R), F32),
            jax.ShapeDtypeStruct((NTS * NE, LANE), I32),
            jax.ShapeDtypeStruct((NE, LANE), I32),
        ],
        scratch_shapes=[pltpu.VMEM((TS, D), F32), pltpu.VMEM((NE, 1), F32)],
        compiler_params=_cp("arbitrary"),
        name="route",
    )(x, g, ada_p, ada_s, ada_p, ada_s, wr_hi, wr_lo, rbias)


U32 = jnp.uint32
HI_MASK = 0xFFFF0000
SEG_CHUNK = 256
COMB_CHUNK = 256
LN_ROWS = 32
CAST_CHUNKS = 8


def _pack_pair(lo, hi):
    lo_bits = lax.shift_right_logical(lax.bitcast_convert_type(lo, U32), jnp.uint32(16))
    hi_bits = lax.bitcast_convert_type(hi, U32) & jnp.uint32(HI_MASK)
    return hi_bits | lo_bits


def _unpack_pair(words):
    lo = lax.bitcast_convert_type(lax.shift_left(words, jnp.uint32(16)), F32)
    hi = lax.bitcast_convert_type(words & jnp.uint32(HI_MASK), F32)
    return lo.astype(BF16), hi.astype(BF16)


def _round_bf16(x):
    return x.astype(BF16).astype(F32)


def _aligned(start, size):
    hint = lambda v: v if isinstance(v, int) else pl.multiple_of(v, SUBLANE)
    return pl.ds(hint(start), hint(size))


def _segment_copies(tile, n8_ref, seg_ref, toff_ref, base_ref, make_copy):
    def per_expert(e, carry):
        idx = tile * NE + e
        n8 = n8_ref[idx]
        tile_row = seg_ref[idx]
        global_row = base_ref[e] + toff_ref[idx]

        @pl.when(n8 > 0)
        def _():
            make_copy(tile_row, global_row, n8).start()

        return carry

    lax.fori_loop(0, NE, per_expert, 0)


def _tile_rows(tile, n8_ref, seg_ref):
    last = tile * NE + NE - 1
    return seg_ref[last] + n8_ref[last]


def _dispatch_kernel(n8_ref, seg_ref, toff_ref, base_ref, cnt_ref, h_ref, slot_ref, xb_ref,
                     s_ref, z_ref, sem, zsem):
    tile = pl.program_id(0)
    buf = tile % 2

    def wait_tile(t):
        rows = _tile_rows(t, n8_ref, seg_ref)
        pltpu.make_async_copy(s_ref.at[t % 2, _aligned(0, rows), :], xb_ref.at[_aligned(0, rows), :],
                              sem.at[t % 2]).wait()

    @pl.when(tile == 0)
    def _():
        z_ref[...] = jnp.zeros_like(z_ref)

        def pad_copy(e):
            n = cnt_ref[e]
            npad = (BLK - n % BLK) % BLK
            return npad, pltpu.make_async_copy(z_ref.at[_aligned(0, npad), :],
                                               xb_ref.at[_aligned(base_ref[e] + n, npad), :], zsem)

        def start(e, carry):
            npad, cp = pad_copy(e)

            @pl.when(npad > 0)
            def _():
                cp.start()

            return carry

        def wait(e, carry):
            npad, cp = pad_copy(e)

            @pl.when(npad > 0)
            def _():
                cp.wait()

            return carry

        lax.fori_loop(0, NE, start, 0)
        lax.fori_loop(0, NE, wait, 0)

    @pl.when(tile >= 2)
    def _():
        wait_tile(tile - 2)

    rows_used = _tile_rows(tile, n8_ref, seg_ref)
    for c in range(SEG // SEG_CHUNK):
        def sort_chunk(c=c):
            rows = c * SEG_CHUNK + lax.broadcasted_iota(I32, (SEG_CHUNK, TS), 0)
            onehot = jnp.zeros((SEG_CHUNK, TS), F32)
            for k in range(TOPK):
                onehot = jnp.where(rows == slot_ref[k:k + 1, :], 1.0, onehot)
            s = jnp.dot(onehot.astype(BF16), h_ref[...], preferred_element_type=F32)
            s_ref[buf, c * SEG_CHUNK:(c + 1) * SEG_CHUNK, :] = _pack_pair(s[:, :DH], s[:, DH:])

        if (c + 1) * SEG_CHUNK <= TOPK * TS:
            sort_chunk()
        else:
            pl.when(c * SEG_CHUNK < rows_used)(sort_chunk)

    def make_copy(tile_row, global_row, size):
        return pltpu.make_async_copy(s_ref.at[buf, _aligned(tile_row, size), :],
                                     xb_ref.at[_aligned(global_row, size), :], sem.at[buf])

    _segment_copies(tile, n8_ref, seg_ref, toff_ref, base_ref, make_copy)

    @pl.when(tile == NTS - 1)
    def _():
        wait_tile(tile - 1)
        wait_tile(tile)


def _dispatch(meta, base8, cnt8, hb, slot):
    n8, seg, toff = meta
    return pl.pallas_call(
        _dispatch_kernel,
        grid_spec=pltpu.PrefetchScalarGridSpec(
            num_scalar_prefetch=5,
            grid=(NTS,),
            in_specs=[pl.BlockSpec((TS, D), lambda i, *_: (i, 0)),
                      pl.BlockSpec((TOPK, TS), lambda i, *_: (0, i))],
            out_specs=pl.BlockSpec(memory_space=pl.ANY),
            scratch_shapes=[pltpu.VMEM((2, SEG, DH), U32), pltpu.VMEM((BLK, DH), U32),
                            pltpu.SemaphoreType.DMA((2,)), pltpu.SemaphoreType.DMA],
        ),
        out_shape=jax.ShapeDtypeStruct((NROWS, DH), U32),
        compiler_params=_cp("arbitrary"),
        name="dispatch",
    )(n8, seg, toff, base8, cnt8, hb, slot)


def _experts_kernel(layer, be_ref, nx_ref, nu_ref, xb_ref, w1_hbm, w3_hbm, w2_hbm, yb_ref,
                    w1s_ref, w3s_ref, w2s_ref, w1b_ref, w3b_ref, w2b_ref, sem):
    b = pl.program_id(0)
    live = b < nu_ref[0]
    e = be_ref[b]
    fresh = (b == 0) | (e != be_ref[jnp.maximum(b - 1, 0)])

    def fetch(expert):
        return [pltpu.make_async_copy(hbm.at[layer, expert], stage, sem.at[j])
                for j, (hbm, stage) in enumerate(((w1_hbm, w1s_ref), (w3_hbm, w3s_ref), (w2_hbm, w2s_ref)))]

    @pl.when(b == 0)
    def _():
        for cp in fetch(e):
            cp.start()

    @pl.when(live & fresh)
    def _():
        for cp in fetch(e):
            cp.wait()
        def cast_chunk(j, carry):
            up_rows = pl.ds(pl.multiple_of(j * (D // CAST_CHUNKS), D // CAST_CHUNKS), D // CAST_CHUNKS)
            down_rows = pl.ds(pl.multiple_of(j * (DE // CAST_CHUNKS), DE // CAST_CHUNKS), DE // CAST_CHUNKS)
            w1b_ref[up_rows, :] = w1s_ref[up_rows, :].astype(BF16)
            w3b_ref[up_rows, :] = w3s_ref[up_rows, :].astype(BF16)
            w2b_ref[down_rows, :] = w2s_ref[down_rows, :].astype(BF16)
            return carry

        lax.fori_loop(0, CAST_CHUNKS, cast_chunk, 0)
        nxt = nx_ref[b]

        @pl.when(nxt != e)
        def _():
            for cp in fetch(nxt):
                cp.start()

    @pl.when(live)
    def _():
        x_lo, x_hi = _unpack_pair(xb_ref[...])

        def up(w_ref):
            return (jnp.dot(x_lo, w_ref[:DH, :], preferred_element_type=F32)
                    + jnp.dot(x_hi, w_ref[DH:, :], preferred_element_type=F32))

        hid = (_silu(up(w1b_ref)) * up(w3b_ref)).astype(BF16)
        for c in range(DH // MXU_N):
            cols = slice(c * MXU_N, (c + 1) * MXU_N)
            y_lo = jnp.dot(hid, w2b_ref[:, c * MXU_N:(c + 1) * MXU_N], preferred_element_type=F32)
            y_hi = jnp.dot(hid, w2b_ref[:, DH + c * MXU_N:DH + (c + 1) * MXU_N], preferred_element_type=F32)
            yb_ref[:, cols] = _pack_pair(_round_bf16(y_lo), _round_bf16(y_hi))


def _experts(blk_e, next_e, n_used, xb, w1, w3, w2, layer):
    row_map = lambda b, be, nx, nu: (jnp.minimum(b, nu[0] - 1), 0)
    hbm = pl.BlockSpec(memory_space=pl.ANY)
    return pl.pallas_call(
        functools.partial(_experts_kernel, layer),
        grid_spec=pltpu.PrefetchScalarGridSpec(
            num_scalar_prefetch=3,
            grid=(NB,),
            in_specs=[pl.BlockSpec((BLK, DH), row_map), hbm, hbm, hbm],
            out_specs=pl.BlockSpec((BLK, DH), row_map),
            scratch_shapes=[pltpu.VMEM((D, DE), F32), pltpu.VMEM((D, DE), F32), pltpu.VMEM((DE, D), F32),
                            pltpu.VMEM((D, DE), BF16), pltpu.VMEM((D, DE), BF16), pltpu.VMEM((DE, D), BF16),
                            pltpu.SemaphoreType.DMA((3,))],
        ),
        out_shape=jax.ShapeDtypeStruct((NROWS, DH), U32),
        compiler_params=_cp("arbitrary"),
        name="experts",
    )(blk_e, next_e, n_used, xb, w1, w3, w2)


def _shared_kernel(h_ref, w1_ref, w3_ref, w2_ref, o_ref):
    x = h_ref[...].astype(BF16)
    hid = (_silu(jnp.dot(x, w1_ref[...], preferred_element_type=F32))
           * jnp.dot(x, w3_ref[...], preferred_element_type=F32))
    o_ref[...] = jnp.dot(hid.astype(BF16), w2_ref[...], preferred_element_type=F32)


def _shared(h2, ws1, ws3, ws2, layer):
    return pl.pallas_call(
        _shared_kernel,
        grid=(NT,),
        in_specs=[pl.BlockSpec((TM, D), lambda i: (i, 0)),
                  _layer_spec((D, DE), layer), _layer_spec((D, DE), layer), _layer_spec((DE, D), layer)],
        out_specs=pl.BlockSpec((TM, D), lambda i: (i, 0)),
        out_shape=jax.ShapeDtypeStruct((R, D), F32),
        compiler_params=_cp("arbitrary"),
        name="shared",
    )(h2, ws1, ws3, ws2)


def _combine_kernel(n8_ref, seg_ref, toff_ref, base_ref, x_ref, sh_ref, slot_ref, wt_ref, gp_ref, gs_ref,
                    gpost_ref, yb_ref, o_ref, ys_ref, f_ref, cm_ref, sem):
    tile = pl.program_id(0)
    buf = tile % 2

    def fetch_tile(t):
        def make_copy(tile_row, global_row, size):
            return pltpu.make_async_copy(yb_ref.at[_aligned(global_row, size), :],
                                         ys_ref.at[t % 2, _aligned(tile_row, size), :], sem.at[t % 2])

        _segment_copies(t, n8_ref, seg_ref, toff_ref, base_ref, make_copy)

    @pl.when(tile == 0)
    def _():
        ys_ref[...] = jnp.zeros_like(ys_ref)
        fetch_tile(tile)

    @pl.when(tile + 1 < NTS)
    def _():
        fetch_tile(tile + 1)

    cols = lax.broadcasted_iota(I32, (TS, SEG), 1)
    cmat = jnp.zeros((TS, SEG), F32)
    for k in range(TOPK):
        cmat = jnp.where(cols == slot_ref[:, k:k + 1], wt_ref[:, k:k + 1], cmat)
    cm_ref[...] = cmat.astype(BF16)

    rows = _tile_rows(tile, n8_ref, seg_ref)
    pltpu.make_async_copy(yb_ref.at[_aligned(0, rows), :], ys_ref.at[buf, _aligned(0, rows), :],
                          sem.at[buf]).wait()

    def weighted(ks):
        y_lo, y_hi = _unpack_pair(ys_ref[buf, ks, :])
        return (jnp.dot(cm_ref[:, ks], y_lo, preferred_element_type=F32),
                jnp.dot(cm_ref[:, ks], y_hi, preferred_element_type=F32))

    lo, hi = weighted(slice(0, TOPK * TS))
    f_ref[:, :DH] = sh_ref[:, :DH] + lo
    f_ref[:, DH:] = sh_ref[:, DH:] + hi
    for start in range(TOPK * TS, SEG, COMB_CHUNK):
        @pl.when(start < rows)
        def _(start=start):
            lo, hi = weighted(slice(start, start + COMB_CHUNK))
            f_ref[:, :DH] += lo
            f_ref[:, DH:] += hi
    gate = jnp.where(tile < RP // TS, gp_ref[...], gs_ref[...])
    gpost = gpost_ref[...]
    for s in range(TS // SUB):
        rows = slice(s * SUB, (s + 1) * SUB)
        o_ref[rows, :] = x_ref[rows, :] + gate * (_rms(f_ref[rows, :]) * gpost)


def _combine(meta, base8, x, shared, slot_t, w_t, ada_p, ada_s, gpost, yb):
    n8, seg, toff = meta
    tiles_per_seq = LP // TS
    gp = pl.BlockSpec((None, 1, D), lambda i, *_: (jnp.minimum(i // tiles_per_seq, NP - 1), 0, 5))
    gs = pl.BlockSpec((NS, D), lambda i, *_: (0, 5))
    row = pl.BlockSpec((TS, D), lambda i, *_: (i, 0))
    tok = pl.BlockSpec((TS, TOPK), lambda i, *_: (i, 0))
    return pl.pallas_call(
        _combine_kernel,
        grid_spec=pltpu.PrefetchScalarGridSpec(
            num_scalar_prefetch=4,
            grid=(NTS,),
            in_specs=[row, row, tok, tok, gp, gs, pl.BlockSpec((1, D), lambda i, *_: (0, 0)),
                      pl.BlockSpec(memory_space=pl.ANY)],
            out_specs=row,
            scratch_shapes=[pltpu.VMEM((2, SEG, DH), U32), pltpu.VMEM((TS, D), F32),
                            pltpu.VMEM((TS, SEG), BF16),
                            pltpu.SemaphoreType.DMA((2,))],
        ),
        out_shape=jax.ShapeDtypeStruct((R, D), F32),
        compiler_params=_cp("arbitrary"),
        name="combine",
    )(n8, seg, toff, base8, x, shared, slot_t, w_t, ada_p, ada_s, gpost, yb)


def _row(v):
    return v.reshape(1, -1)


def _to_rows(x_prompt, x_sample):
    return jnp.concatenate([x_prompt.reshape(RP, D), x_sample.transpose(1, 0, 2).reshape(RS, D)], axis=0)


def _sample_rows_to_batch(rows, width):
    return rows.reshape(LS, NS, width).transpose(1, 0, 2)


def kernel(x_prompt, x_sample, state_conv, state_pool, c_prompt, c_sample, w_ada, b_ada, g_pre_mix,
           g_post_mix, g_pre_ffn, g_post_ffn, w_in, conv_w, conv_b, conv_ln_g, conv_ln_b, conv_w_pw,
           gmlp_ln_g, gmlp_ln_b, gmlp_w_s, gmlp_b_s, gmlp_w_o, pool_w, pool_scale, w_out, router_w,
           router_bias, w1, w3, w2, ws1, ws3, ws2):
    x = _to_rows(x_prompt, x_sample)
    ada = _ada(jnp.concatenate([c_prompt, c_sample], axis=0), w_ada, b_ada)
    wpw_b, wo_b, wpool_b, wout_b = (w.astype(BF16) for w in (conv_w_pw, gmlp_w_o, pool_w, w_out))
    ws1_b, ws3_b, ws2_b = (w.astype(BF16) for w in (ws1, ws3, ws2))
    expert_ids = jnp.arange(NE, dtype=I32)

    def block_expert(pad_end, first_row):
        return jnp.minimum(jnp.sum((pad_end[None, :] <= first_row[:, None]).astype(I32), axis=1), NE - 1)

    conv_p, pool_p, conv_s, pool_s, v_s = [], [], [], [], []
    for l in range(DEPTH):
        ada_p = ada[l, :NP].reshape(NP, 1, 6 * D)
        ada_s = ada[l, NP:]

        h = _hmod(x, _row(g_pre_mix[l]), ada_p, ada_s, 1, 0)
        glu = _colmm(h, w_in, l, (0, DC), DC, 512, _glu_epilogue, F32, "in_glu")
        zb = _colmm(h, w_in, l, (COL_GMLP,), 2 * DC, 1024, _gelu_epilogue, F32, "in_gelu")
        p = _colmm(h, w_in, l, (COL_POOL,), DC, 1024, _id_epilogue, F32, "in_pool")
        gates = _colmm(h, w_in, l, (COL_GATE,), 3 * D, 1024, _sigmoid_epilogue, BF16, "in_gate")

        cw, cb = conv_w[l], _row(conv_b[l])
        clg, clb = _row(conv_ln_g[l]), _row(conv_ln_b[l])
        a_p = _conv_p(glu, cw, cb, clg, clb)
        a_s = _conv_s(glu, state_conv[l].transpose(1, 0, 2), cw, cb, clg, clb)

        glg, glb = _row(gmlp_ln_g[l]), _row(gmlp_ln_b[l])
        b_p = _gmlp_p(zb, gmlp_w_s[l], gmlp_b_s[l].T, glg, glb)
        coef = jnp.repeat(gmlp_w_s[l][:, :LS, :LS].transpose(1, 2, 0).reshape(LS * LS, HEADS), DC // HEADS, axis=1)
        bias = jnp.repeat(gmlp_b_s[l][:, :LS].T, DC // HEADS, axis=1)
        b_s, v_rows = _gmlp_s(zb, coef, bias, glg, glb)

        c_p = _pool_p(p)
        c_s = _pool_s(p, state_pool[l].transpose(1, 0, 2))

        x = _merge(((a_p, a_s), (b_p, b_s), (c_p, c_s)), gates, x, ada_p, ada_s, _row(g_post_mix[l]),
                   wpw_b, wo_b, wpool_b, _row(pool_scale[l]), wout_b, l)

        conv_p.append(jnp.stack([glu[(b + 1) * LP - HIST_C:(b + 1) * LP] for b in range(NP)]))
        pool_p.append(jnp.stack([p[(b + 1) * LP - HIST_P:(b + 1) * LP] for b in range(NP)]))
        conv_s.append(jnp.concatenate([state_conv[l][:, LS:], _sample_rows_to_batch(glu[RP:], DC)], axis=1))
        pool_s.append(jnp.concatenate([state_pool[l][:, LS:], _sample_rows_to_batch(p[RP:], DC)], axis=1))
        v_s.append(_sample_rows_to_batch(v_rows, DC))

        wr_t = router_w[l].T
        wr_hi = wr_t.astype(BF16)
        wr_lo = (wr_t - wr_hi.astype(F32)).astype(BF16)
        hb, slot, w_k, meta, cnt = _route(x, _row(g_pre_ffn[l]), ada_p, ada_s, wr_hi, wr_lo,
                                          router_bias[l].reshape(NE, 1))
        meta = tuple(meta[:, c] for c in range(3))
        cnt8 = cnt[:, 0]
        padded = (cnt8 + BLK - 1) // BLK * BLK
        pad_end = jnp.cumsum(padded)
        base8 = (pad_end - padded).astype(I32)
        n_used = (pad_end[-1] // BLK).astype(I32)
        blk = jnp.minimum(jnp.arange(NB, dtype=I32), n_used - 1)
        blk_e = block_expert(pad_end, blk * BLK)
        own_end = jnp.sum(jnp.where(expert_ids[None, :] == blk_e[:, None], pad_end[None, :], 0), axis=1)
        next_blk = jnp.minimum(own_end // BLK, n_used - 1)
        next_e = block_expert(pad_end, next_blk * BLK)

        xb = _dispatch(meta, base8, cnt8, hb, slot)
        yb = _experts(blk_e, next_e, n_used.reshape(1), xb, w1, w3, w2, l)
        shared = _shared(hb, ws1_b, ws3_b, ws2_b, l)
        x = _combine(meta, base8, x, shared, slot.T, w_k.T, ada_p, ada_s, _row(g_post_ffn[l]), yb)

    y_prompt = x[:RP].reshape(NP, LP, D)
    y_sample = _sample_rows_to_batch(x[RP:], D)
    return (y_prompt, y_sample, jnp.stack(conv_p), jnp.stack(pool_p), jnp.stack(conv_s),
            jnp.stack(pool_s), jnp.stack(v_s))
```

```python
import functools

import jax
import jax.numpy as jnp
from jax import lax
from jax.experimental import pallas as pl
from jax.experimental.pallas import tpu as pltpu

F32 = jnp.float32
BF16 = jnp.bfloat16
I32 = jnp.int32

D = 2048
NP, LP = 4, 2048
NS, LS = 128, 4
RP = NP * LP
RS = NS * LS
R = RP + RS
DEPTH = 2
DC = 1024
CONV_W = 31
HIST_C = CONV_W - 1
HIST_P = 15
POOL_WINDOWS = (2, 4, 8, 16)
HEADS = 8
CHUNK = 128
D_IN = 5 * DC + 3 * D
COL_GMLP, COL_POOL, COL_GATE = 2 * DC, 4 * DC, 5 * DC
NE, TOPK, NGRP, TOPG = 64, 8, 8, 4
DE = 512
ROUTE_SCALE = 2.5
EPS = 1e-6

TM = 512
NT = R // TM
NTP = RP // TM
SUB = 128
NSUB = TM // SUB
TMM = 256
TMP = R // 8
LANE = 128
SUBLANE = 8
MXU_N = 256
TS = 256
NTS = R // TS
SEG = TOPK * TS + NE * SUBLANE
DH = D // 2
BLK = 256
NB = -(-(R * TOPK + NTS * NE * (SUBLANE - 1) + NE * (BLK - 1)) // BLK)
NROWS = NB * BLK

VMEM_LIMIT = 56 * 1024 * 1024


def _cp(*sem):
    return pltpu.CompilerParams(dimension_semantics=sem, vmem_limit_bytes=VMEM_LIMIT)


def _const_spec(shape):
    nd = len(shape)
    return pl.BlockSpec(shape, lambda *_: (0,) * nd, pipeline_mode=pl.Buffered(1))


def _layer_spec(shape, layer):
    nd = len(shape)
    return pl.BlockSpec((None,) + tuple(shape), lambda *_: (layer,) + (0,) * nd, pipeline_mode=pl.Buffered(1))


def _rms(x):
    return x * lax.rsqrt(jnp.mean(x * x, axis=-1, keepdims=True) + EPS)


def _ln(x, g, b):
    mu = jnp.mean(x, axis=-1, keepdims=True)
    xc = x - mu
    var = jnp.mean(xc * xc, axis=-1, keepdims=True)
    return xc * lax.rsqrt(var + EPS) * g + b


def _silu(x):
    return x * jax.nn.sigmoid(x)


ADA_TN = 1024


def _ada_kernel(c_ref, w_ref, b_ref, o_ref):
    c = _silu(c_ref[...]).astype(BF16)
    o_ref[...] = jnp.dot(c, w_ref[...].astype(BF16), preferred_element_type=F32) + b_ref[...]


def _ada(c_all, w_ada, b_ada):
    n = c_all.shape[0]
    return pl.pallas_call(
        _ada_kernel,
        grid=(DEPTH, 6 * D // ADA_TN),
        in_specs=[
            pl.BlockSpec((n, D), lambda l, j: (0, 0)),
            pl.BlockSpec((None, D, ADA_TN), lambda l, j: (l, 0, j)),
            pl.BlockSpec((None, 1, ADA_TN), lambda l, j: (l, 0, j)),
        ],
        out_specs=pl.BlockSpec((None, n, ADA_TN), lambda l, j: (l, 0, j)),
        out_shape=jax.ShapeDtypeStruct((DEPTH, n, 6 * D), F32),
        compiler_params=_cp("arbitrary", "arbitrary"),
        name="ada",
    )(c_all, w_ada, b_ada.reshape(DEPTH, 1, 6 * D))


def _ada_specs(chunk, tile_rows):
    tiles_per_seq = LP // tile_rows
    p = pl.BlockSpec((None, 1, D), lambda i: (jnp.minimum(i // tiles_per_seq, NP - 1), 0, chunk))
    s = pl.BlockSpec((NS, D), lambda i: (0, chunk))
    return p, s


def _hmod_kernel(x_ref, g_ref, scp_ref, scs_ref, shp_ref, shs_ref, o_ref):
    is_p = pl.program_id(0) < NTP
    sc = 1.0 + jnp.where(is_p, scp_ref[...], scs_ref[...])
    sh = jnp.where(is_p, shp_ref[...], shs_ref[...])
    g = g_ref[...]
    for s in range(NSUB):
        rows = slice(s * SUB, (s + 1) * SUB)
        o_ref[rows, :] = (_rms(x_ref[rows, :]) * g * sc + sh).astype(o_ref.dtype)


def _hmod(x, g, ada_p, ada_s, c_scale, c_shift):
    scp, scs = _ada_specs(c_scale, TM)
    shp, shs = _ada_specs(c_shift, TM)
    return pl.pallas_call(
        _hmod_kernel,
        grid=(NT,),
        in_specs=[pl.BlockSpec((TM, D), lambda i: (i, 0)), _const_spec((1, D)), scp, scs, shp, shs],
        out_specs=pl.BlockSpec((TM, D), lambda i: (i, 0)),
        out_shape=jax.ShapeDtypeStruct((R, D), BF16),
        compiler_params=_cp("arbitrary"),
        name="hmod",
    )(x, g, ada_p, ada_s, ada_p, ada_s)


def _colmm_kernel(nw, epilogue, *refs):
    lhs_ref = refs[0]
    w_refs = refs[1:1 + nw]
    o_ref = refs[1 + nw]
    wb_refs = refs[2 + nw:]

    @pl.when(pl.program_id(1) == 0)
    def _():
        for w_ref, wb_ref in zip(w_refs, wb_refs):
            wb_ref[...] = w_ref[...].astype(BF16)

    lhs = lhs_ref[...]
    accs = [jnp.dot(lhs, wb_ref[...], preferred_element_type=F32) for wb_ref in wb_refs]
    o_ref[...] = epilogue(*accs).astype(o_ref.dtype)


def _colmm(h, w_in, layer, col_offsets, n_cols, tn, epilogue, out_dtype, name):
    nw = len(col_offsets)
    w_specs = [
        pl.BlockSpec((None, D, tn), functools.partial(lambda j, i, o: (layer, 0, o + j), o=off // tn))
        for off in col_offsets
    ]
    return pl.pallas_call(
        functools.partial(_colmm_kernel, nw, epilogue),
        grid=(n_cols // tn, R // TMP),
        in_specs=[pl.BlockSpec((TMP, D), lambda j, i: (i, 0))] + w_specs,
        out_specs=pl.BlockSpec((TMP, tn), lambda j, i: (i, j)),
        out_shape=jax.ShapeDtypeStruct((R, n_cols), out_dtype),
        scratch_shapes=[pltpu.VMEM((D, tn), BF16) for _ in range(nw)],
        compiler_params=_cp("arbitrary", "arbitrary"),
        name=name,
    )(h, *([w_in] * nw))


def _glu_epilogue(a, g):
    return a * jax.nn.sigmoid(g)


def _gelu_epilogue(z):
    return 0.5 * z * (1.0 + lax.erf(z * (0.5 ** 0.5)))


def _id_epilogue(z):
    return z


def _sigmoid_epilogue(z):
    return jax.nn.sigmoid(z)


CONV_HALO = 32


def _conv_taps(w_ref, slab, cs):
    acc = None
    for k in range(CONV_W):
        term = w_ref[k:k + 1, cs] * slab(k)
        acc = term if acc is None else acc + term
    return acc


def _conv_p_kernel(main_ref, halo_ref, w_ref, b_ref, lg_ref, lb_ref, o_ref, xs_ref, y_ref):
    first = (pl.program_id(0) % (LP // SUB)) == 0
    xs_ref[0, 0:CONV_HALO, :] = jnp.where(first, 0.0, halo_ref[...])
    xs_ref[0, CONV_HALO:, :] = main_ref[...]
    shift_rows = CONV_HALO + SUB - SUBLANE
    for s in range(1, SUBLANE):
        xs_ref[s, 0:shift_rows, :] = xs_ref[0, s:s + shift_rows, :]
    base = CONV_HALO - HIST_C

    def window(k, cs):
        s = (base + k) % SUBLANE
        start = base + k - s
        return xs_ref[s, start:start + SUB, cs]

    def lane_chunk(c, carry):
        cs = pl.ds(pl.multiple_of(c * LANE, LANE), LANE)
        acc = _conv_taps(w_ref, functools.partial(window, cs=cs), cs)
        y_ref[:, cs] = acc + b_ref[:, cs]
        return carry

    lax.fori_loop(0, DC // LANE, lane_chunk, 0)
    for r in range(SUB // LN_ROWS):
        rs = slice(r * LN_ROWS, (r + 1) * LN_ROWS)
        o_ref[rs, :] = _silu(_ln(y_ref[rs, :], lg_ref[...], lb_ref[...])).astype(o_ref.dtype)


def _conv_p(glu, w, b, lg, lb):
    halo_per_tile = SUB // CONV_HALO
    return pl.pallas_call(
        _conv_p_kernel,
        grid=(RP // SUB,),
        in_specs=[
            pl.BlockSpec((SUB, DC), lambda i: (i, 0)),
            pl.BlockSpec((CONV_HALO, DC), lambda i: (jnp.maximum(i * halo_per_tile - 1, 0), 0)),
            _const_spec((CONV_W, DC)), _const_spec((1, DC)), _const_spec((1, DC)), _const_spec((1, DC)),
        ],
        out_specs=pl.BlockSpec((SUB, DC), lambda i: (i, 0)),
        out_shape=jax.ShapeDtypeStruct((RP, DC), BF16),
        scratch_shapes=[pltpu.VMEM((SUBLANE, CONV_HALO + SUB, DC), F32), pltpu.VMEM((SUB, DC), F32)],
        compiler_params=_cp("arbitrary"),
        name="conv_prompt",
    )(glu, glu, w, b, lg, lb)


def _conv_s_kernel(glu_ref, st_ref, w_ref, b_ref, lg_ref, lb_ref, o_ref, y_ref):
    for l in range(LS):
        rows = slice(l * SUB, (l + 1) * SUB)

        def slab(k, cs):
            j = l + k
            if j < HIST_C:
                return st_ref[j, :, cs]
            return glu_ref[(j - HIST_C) * SUB:(j - HIST_C + 1) * SUB, cs]

        for c in range(DC // LANE):
            cs = slice(c * LANE, (c + 1) * LANE)
            acc = _conv_taps(w_ref, functools.partial(slab, cs=cs), cs)
            y_ref[:, cs] = acc + b_ref[:, cs]
        o_ref[rows, :] = _silu(_ln(y_ref[...], lg_ref[...], lb_ref[...])).astype(o_ref.dtype)


def _conv_s(glu, state_t, w, b, lg, lb):
    return pl.pallas_call(
        _conv_s_kernel,
        grid=(1,),
        in_specs=[
            pl.BlockSpec((RS, DC), lambda i: (RP // RS, 0)),
            _const_spec((HIST_C, NS, DC)),
            _const_spec((CONV_W, DC)), _const_spec((1, DC)), _const_spec((1, DC)), _const_spec((1, DC)),
        ],
        out_specs=_const_spec((RS, DC)),
        out_shape=jax.ShapeDtypeStruct((RS, DC), BF16),
        scratch_shapes=[pltpu.VMEM((SUB, DC), F32)],
        compiler_params=_cp("arbitrary"),
        name="conv_sample",
    )(glu, state_t, w, b, lg, lb)


def _gmlp_p_kernel(u_ref, v_ref, ws_ref, bst_ref, lg_ref, lb_ref, o_ref, wsb_ref):
    tgt = lax.broadcasted_iota(I32, (CHUNK, CHUNK), 0)
    src = lax.broadcasted_iota(I32, (CHUNK, CHUNK), 1)
    causal = src <= tgt
    hd = DC // HEADS
    for h in range(HEADS):
        wsb_ref[h] = jnp.where(causal, ws_ref[h], 0.0).astype(BF16)
    for n in range(TM // CHUNK):
        rows = slice(n * CHUNK, (n + 1) * CHUNK)
        v = _ln(v_ref[rows, :], lg_ref[...], lb_ref[...]).astype(BF16)
        for h in range(HEADS):
            cs = slice(h * hd, (h + 1) * hd)
            mixed = jnp.dot(wsb_ref[h], v[:, cs], preferred_element_type=F32) + bst_ref[:, h:h + 1]
            o_ref[rows, cs] = (u_ref[rows, cs] * mixed).astype(o_ref.dtype)


def _gmlp_p(zb, ws, bst, lg, lb):
    return pl.pallas_call(
        _gmlp_p_kernel,
        grid=(RP // TM,),
        in_specs=[
            pl.BlockSpec((TM, DC), lambda i: (i, 0)),
            pl.BlockSpec((TM, DC), lambda i: (i, 1)),
            _const_spec((HEADS, CHUNK, CHUNK)), _const_spec((CHUNK, HEADS)),
            _const_spec((1, DC)), _const_spec((1, DC)),
        ],
        out_specs=pl.BlockSpec((TM, DC), lambda i: (i, 0)),
        scratch_shapes=[pltpu.VMEM((HEADS, CHUNK, CHUNK), BF16)],
        out_shape=jax.ShapeDtypeStruct((RP, DC), BF16),
        compiler_params=_cp("arbitrary"),
        name="gmlp_prompt",
    )(zb, zb, ws, bst, lg, lb)


def _gmlp_s_kernel(u_ref, v_ref, coef_ref, bias_ref, lg_ref, lb_ref, o_ref, vo_ref):
    for l in range(LS):
        rows = slice(l * SUB, (l + 1) * SUB)
        vo_ref[rows, :] = _ln(v_ref[rows, :], lg_ref[...], lb_ref[...])
    for l in range(LS):
        rows = slice(l * SUB, (l + 1) * SUB)
        mixed = bias_ref[l:l + 1, :]
        for s in range(l + 1):
            mixed = mixed + coef_ref[l * LS + s:l * LS + s + 1, :] * vo_ref[s * SUB:(s + 1) * SUB, :]
        o_ref[rows, :] = (u_ref[rows, :] * mixed).astype(o_ref.dtype)


def _gmlp_s(zb, coef, bias, lg, lb):
    return pl.pallas_call(
        _gmlp_s_kernel,
        grid=(1,),
        in_specs=[
            pl.BlockSpec((RS, DC), lambda i: (RP // RS, 0)),
            pl.BlockSpec((RS, DC), lambda i: (RP // RS, 1)),
            _const_spec((LS * LS, DC)), _const_spec((LS, DC)), _const_spec((1, DC)), _const_spec((1, DC)),
        ],
        out_specs=[_const_spec((RS, DC)), _const_spec((RS, DC))],
        out_shape=[jax.ShapeDtypeStruct((RS, DC), BF16), jax.ShapeDtypeStruct((RS, DC), F32)],
        compiler_params=_cp("arbitrary"),
        name="gmlp_sample",
    )(zb, zb, coef, bias, lg, lb)


POOL_HALO = 16
GD = DC // len(POOL_WINDOWS)


def _pool_p_kernel(main_ref, halo_ref, o_ref, xc_ref):
    t_tile = pl.program_id(0) % (LP // TM)
    xc_ref[0:POOL_HALO, :] = jnp.where(t_tile == 0, 0.0, halo_ref[...])
    xc_ref[POOL_HALO:, :] = main_ref[...]
    for n in range(TM // SUB):
        first = POOL_HALO + n * SUB
        pos = t_tile * TM + n * SUB + lax.broadcasted_iota(I32, (SUB, 1), 0)
        for g, w in enumerate(POOL_WINDOWS):
            cs = slice(g * GD, (g + 1) * GD)
            tok = xc_ref[first:first + SUB, cs]
            acc = tok
            for j in range(1, w):
                acc = acc + xc_ref[first - j:first - j + SUB, cs]
            cnt = jnp.minimum(pos + 1, w).astype(F32)
            o_ref[n * SUB:(n + 1) * SUB, cs] = (acc / cnt - tok).astype(o_ref.dtype)


def _pool_p(p):
    halo_per_tile = TM // POOL_HALO
    return pl.pallas_call(
        _pool_p_kernel,
        grid=(RP // TM,),
        in_specs=[
            pl.BlockSpec((TM, DC), lambda i: (i, 0)),
            pl.BlockSpec((POOL_HALO, DC), lambda i: (jnp.maximum(i * halo_per_tile - 1, 0), 0)),
        ],
        out_specs=pl.BlockSpec((TM, DC), lambda i: (i, 0)),
        out_shape=jax.ShapeDtypeStruct((RP, DC), BF16),
        scratch_shapes=[pltpu.VMEM((POOL_HALO + TM, DC), F32)],
        compiler_params=_cp("arbitrary"),
        name="pool_prompt",
    )(p, p)


def _pool_s_kernel(p_ref, st_ref, o_ref):
    def slab(j, cs):
        if j < HIST_P:
            return st_ref[j, :, cs]
        return p_ref[(j - HIST_P) * SUB:(j - HIST_P + 1) * SUB, cs]

    for l in range(LS):
        for g, w in enumerate(POOL_WINDOWS):
            cs = slice(g * GD, (g + 1) * GD)
            tok = slab(HIST_P + l, cs)
            acc = tok
            for j in range(1, w):
                acc = acc + slab(HIST_P + l - j, cs)
            o_ref[l * SUB:(l + 1) * SUB, cs] = (acc / float(w) - tok).astype(o_ref.dtype)


def _pool_s(p, state_t):
    return pl.pallas_call(
        _pool_s_kernel,
        grid=(1,),
        in_specs=[pl.BlockSpec((RS, DC), lambda i: (RP // RS, 0)), _const_spec((HIST_P, NS, DC))],
        out_specs=_const_spec((RS, DC)),
        out_shape=jax.ShapeDtypeStruct((RS, DC), BF16),
        compiler_params=_cp("arbitrary"),
        name="pool_sample",
    )(p, state_t)


def _merge_kernel(ap_ref, as_ref, bp_ref, bs_ref, cp_ref, cs_ref, ga_ref, gb_ref, gc_ref, x_ref,
                  gp_ref, gs_ref, gpost_ref, wpw_ref, wo_ref, wpool_ref, pscale_ref, wout_ref,
                  o_ref, m_ref):
    is_p = pl.program_id(0) < RP // TMM
    a = jnp.where(is_p, ap_ref[...], as_ref[...])
    b = jnp.where(is_p, bp_ref[...], bs_ref[...])
    c = jnp.where(is_p, cp_ref[...], cs_ref[...])
    m_ref[...] = ga_ref[...].astype(F32) * jnp.dot(a, wpw_ref[...], preferred_element_type=F32)
    m_ref[...] += gb_ref[...].astype(F32) * jnp.dot(b, wo_ref[...], preferred_element_type=F32)
    eo = D // len(POOL_WINDOWS)
    for g in range(len(POOL_WINDOWS)):
        os_ = slice(g * eo, (g + 1) * eo)
        yc = jnp.dot(c[:, g * GD:(g + 1) * GD], wpool_ref[g], preferred_element_type=F32)
        m_ref[:, os_] += gc_ref[:, os_].astype(F32) * (yc * pscale_ref[:, os_])
    mix = jnp.dot(m_ref[...].astype(BF16), wout_ref[...], preferred_element_type=F32)
    m_ref[...] = _rms(mix) * gpost_ref[...]
    gate = jnp.where(is_p, gp_ref[...], gs_ref[...])
    for s in range(TMM // SUB):
        rows = slice(s * SUB, (s + 1) * SUB)
        o_ref[rows, :] = x_ref[rows, :] + gate * m_ref[rows, :]


def _merge(acts, gates, x, ada_p, ada_s, gpost, wpw, wo, wpool, pscale, wout, layer):
    (a_p, a_s), (b_p, b_s), (c_p, c_s) = acts
    ntp = RP // TMM
    pspec = pl.BlockSpec((TMM, DC), lambda i: (jnp.minimum(i, ntp - 1), 0))
    sspec = pl.BlockSpec((TMM, DC), lambda i: (jnp.maximum(i - ntp, 0), 0))
    gp, gs = _ada_specs(2, TMM)
    return pl.pallas_call(
        _merge_kernel,
        grid=(R // TMM,),
        in_specs=[
            pspec, sspec, pspec, sspec, pspec, sspec,
            pl.BlockSpec((TMM, D), lambda i: (i, 0)),
            pl.BlockSpec((TMM, D), lambda i: (i, 1)),
            pl.BlockSpec((TMM, D), lambda i: (i, 2)),
            pl.BlockSpec((TMM, D), lambda i: (i, 0)),
            gp, gs, _const_spec((1, D)),
            _layer_spec((DC, D), layer), _layer_spec((DC, D), layer),
            _layer_spec((len(POOL_WINDOWS), GD, D // 4), layer),
            _const_spec((1, D)), _layer_spec((D, D), layer),
        ],
        out_specs=pl.BlockSpec((TMM, D), lambda i: (i, 0)),
        out_shape=jax.ShapeDtypeStruct((R, D), F32),
        scratch_shapes=[pltpu.VMEM((TMM, D), F32)],
        compiler_params=_cp("arbitrary"),
        name="merge",
    )(a_p, a_s, b_p, b_s, c_p, c_s, gates, gates, gates, x, ada_p, ada_s, gpost,
      wpw, wo, wpool, pscale, wout)


def _route_kernel(x_ref, g_ref, scp_ref, scs_ref, shp_ref, shs_ref, wrh_ref, wrl_ref, rb_ref,
                  hb_ref, slot_ref, w_ref, meta_ref, cnt_ref, h_ref, carry_ref):
    TM = TS
    i = pl.program_id(0)

    @pl.when(i == 0)
    def _():
        carry_ref[...] = jnp.zeros_like(carry_ref)

    is_p = i < RP // TS
    sc = 1.0 + jnp.where(is_p, scp_ref[...], scs_ref[...])
    sh = jnp.where(is_p, shp_ref[...], shs_ref[...])
    g = g_ref[...]
    for s in range(TS // SUB):
        rows = slice(s * SUB, (s + 1) * SUB)
        h_ref[rows, :] = _rms(x_ref[rows, :]) * g * sc + sh

    h = h_ref[...]
    h_hi = h.astype(BF16)
    hb_ref[...] = h_hi
    h_lo = (h - h_hi.astype(F32)).astype(BF16)
    nt = (((1,), (1,)), ((), ()))
    logits = (lax.dot_general(wrh_ref[...], h_hi, nt, preferred_element_type=F32)
              + lax.dot_general(wrl_ref[...], h_hi, nt, preferred_element_type=F32)
              + lax.dot_general(wrh_ref[...], h_lo, nt, preferred_element_type=F32))
    scores = jax.nn.sigmoid(logits)
    biased = scores + rb_ref[...]

    ge = NE // NGRP
    g3 = biased.reshape(NGRP, ge, TM)
    idx3 = lax.broadcasted_iota(I32, (NGRP, ge, TM), 1)
    m1 = jnp.max(g3, axis=1, keepdims=True)
    first = jnp.min(jnp.where(g3 == m1, idx3, ge), axis=1, keepdims=True)
    m2 = jnp.max(jnp.where(idx3 == first, -jnp.inf, g3), axis=1, keepdims=True)
    gscore = (m1 + m2).reshape(NGRP, TM)

    gidx = lax.broadcasted_iota(I32, (NGRP, TM), 0)
    grank = jnp.zeros((NGRP, TM), F32)
    for j in range(NGRP):
        sj = gscore[j:j + 1, :]
        grank = grank + jnp.where((sj > gscore) | ((sj == gscore) & (j < gidx)), 1.0, 0.0)
    gkeep = grank < TOPG
    ekeep = jnp.broadcast_to(gkeep.reshape(NGRP, 1, TM), (NGRP, ge, TM)).reshape(NE, TM)
    masked = jnp.where(ekeep, biased, -jnp.inf)

    eidx = lax.broadcasted_iota(I32, (NE, TM), 0)
    erank = jnp.zeros((NE, TM), F32)
    for j in range(NE):
        sj = masked[j:j + 1, :]
        erank = erank + jnp.where((sj > masked) | ((sj == masked) & (j < eidx)), 1.0, 0.0)
    sel = erank < TOPK

    sw = jnp.where(sel, scores, 0.0)
    cw = sw / jnp.sum(sw, axis=0, keepdims=True) * ROUTE_SCALE

    self_ = jnp.where(sel, 1.0, 0.0)
    t_src = lax.broadcasted_iota(I32, (TS, TS), 0)
    t_dst = lax.broadcasted_iota(I32, (TS, TS), 1)
    before = jnp.where(t_src < t_dst, 1.0, 0.0).astype(BF16)
    rank = jnp.dot(self_.astype(BF16), before, preferred_element_type=F32)
    n8 = jnp.floor((jnp.sum(self_, axis=1, keepdims=True) + (SUBLANE - 1)) / SUBLANE) * SUBLANE
    n8b = jnp.broadcast_to(n8, (NE, LANE))
    e_src = lax.broadcasted_iota(I32, (NE, NE), 1)
    e_dst = lax.broadcasted_iota(I32, (NE, NE), 0)
    lower = jnp.where(e_src < e_dst, 1.0, 0.0).astype(BF16)
    segoff = jnp.dot(lower, n8b.astype(BF16), preferred_element_type=F32)
    slot_dense = segoff[:, 0:1] + rank

    lane = lax.broadcasted_iota(I32, (NE, LANE), 1)
    tileoff = jnp.broadcast_to(carry_ref[...], (NE, LANE))
    meta_ref[...] = jnp.where(lane == 0, n8b, jnp.where(lane == 1, segoff, tileoff)).astype(I32)
    carry_ref[...] += n8
    cnt_ref[...] = jnp.broadcast_to(carry_ref[...], cnt_ref.shape).astype(I32)

    rem = sel
    for k in range(TOPK):
        ek = jnp.min(jnp.where(rem, eidx, NE), axis=0, keepdims=True)
        hit = eidx == ek
        slot_ref[k:k + 1, :] = jnp.sum(jnp.where(hit, slot_dense, 0.0), axis=0, keepdims=True).astype(I32)
        w_ref[k:k + 1, :] = jnp.sum(jnp.where(hit, cw, 0.0), axis=0, keepdims=True)
        rem = rem & jnp.logical_not(hit)


def _route(x, g, ada_p, ada_s, wr_hi, wr_lo, rbias):
    scp, scs = _ada_specs(4, TS)
    shp, shs = _ada_specs(3, TS)
    tok_spec = pl.BlockSpec((TOPK, TS), lambda i: (0, i))
    return pl.pallas_call(
        _route_kernel,
        grid=(NTS,),
        in_specs=[
            pl.BlockSpec((TS, D), lambda i: (i, 0)), _const_spec((1, D)), scp, scs, shp, shs,
            _const_spec((NE, D)), _const_spec((NE, D)), _const_spec((NE, 1)),
        ],
        out_specs=[pl.BlockSpec((TS, D), lambda i: (i, 0)), tok_spec, tok_spec,
                   pl.BlockSpec((NE, LANE), lambda i: (i, 0)), _const_spec((NE, LANE))],
        out_shape=[
            jax.ShapeDtypeStruct((R, D), BF16),
            jax.ShapeDtypeStruct((TOPK, R), I32),
            jax.ShapeDtypeStruct((TOPK, R), F32),
            jax.ShapeDtypeStruct((NTS * NE, LANE), I32),
            jax.ShapeDtypeStruct((NE, LANE), I32),
        ],
        scratch_shapes=[pltpu.VMEM((TS, D), F32), pltpu.VMEM((NE, 1), F32)],
        compiler_params=_cp("arbitrary"),
        name="route",
    )(x, g, ada_p, ada_s, ada_p, ada_s, wr_hi, wr_lo, rbias)


U32 = jnp.uint32
HI_MASK = 0xFFFF0000
SEG_CHUNK = 256
COMB_CHUNK = 256
LN_ROWS = 32
CAST_CHUNKS = 8


def _pack_pair(lo, hi):
    lo_bits = lax.shift_right_logical(lax.bitcast_convert_type(lo, U32), jnp.uint32(16))
    hi_bits = lax.bitcast_convert_type(hi, U32) & jnp.uint32(HI_MASK)
    return hi_bits | lo_bits


def _unpack_pair(words):
    lo = lax.bitcast_convert_type(lax.shift_left(words, jnp.uint32(16)), F32)
    hi = lax.bitcast_convert_type(words & jnp.uint32(HI_MASK), F32)
    return lo.astype(BF16), hi.astype(BF16)


def _round_bf16(x):
    return x.astype(BF16).astype(F32)


def _aligned(start, size):
    hint = lambda v: v if isinstance(v, int) else pl.multiple_of(v, SUBLANE)
    return pl.ds(hint(start), hint(size))


def _segment_copies(tile, n8_ref, seg_ref, toff_ref, base_ref, make_copy):
    def per_expert(e, carry):
        idx = tile * NE + e
        n8 = n8_ref[idx]
        tile_row = seg_ref[idx]
        global_row = base_ref[e] + toff_ref[idx]

        @pl.when(n8 > 0)
        def _():
            make_copy(tile_row, global_row, n8).start()

        return carry

    lax.fori_loop(0, NE, per_expert, 0)


def _tile_rows(tile, n8_ref, seg_ref):
    last = tile * NE + NE - 1
    return seg_ref[last] + n8_ref[last]


def _dispatch_kernel(n8_ref, seg_ref, toff_ref, base_ref, cnt_ref, h_ref, slot_ref, xb_ref,
                     s_ref, z_ref, sem, zsem):
    tile = pl.program_id(0)
    buf = tile % 2

    def wait_tile(t):
        rows = _tile_rows(t, n8_ref, seg_ref)
        pltpu.make_async_copy(s_ref.at[t % 2, _aligned(0, rows), :], xb_ref.at[_aligned(0, rows), :],
                              sem.at[t % 2]).wait()

    @pl.when(tile == 0)
    def _():
        z_ref[...] = jnp.zeros_like(z_ref)

        def pad_copy(e):
            n = cnt_ref[e]
            npad = (BLK - n % BLK) % BLK
            return npad, pltpu.make_async_copy(z_ref.at[_aligned(0, npad), :],
                                               xb_ref.at[_aligned(base_ref[e] + n, npad), :], zsem)

        def start(e, carry):
            npad, cp = pad_copy(e)

            @pl.when(npad > 0)
            def _():
                cp.start()

            return carry

        def wait(e, carry):
            npad, cp = pad_copy(e)

            @pl.when(npad > 0)
            def _():
                cp.wait()

            return carry

        lax.fori_loop(0, NE, start, 0)
        lax.fori_loop(0, NE, wait, 0)

    @pl.when(tile >= 2)
    def _():
        wait_tile(tile - 2)

    rows_used = _tile_rows(tile, n8_ref, seg_ref)
    for c in range(SEG // SEG_CHUNK):
        def sort_chunk(c=c):
            rows = c * SEG_CHUNK + lax.broadcasted_iota(I32, (SEG_CHUNK, TS), 0)
            onehot = jnp.zeros((SEG_CHUNK, TS), F32)
            for k in range(TOPK):
                onehot = jnp.where(rows == slot_ref[k:k + 1, :], 1.0, onehot)
            s = jnp.dot(onehot.astype(BF16), h_ref[...], preferred_element_type=F32)
            s_ref[buf, c * SEG_CHUNK:(c + 1) * SEG_CHUNK, :] = _pack_pair(s[:, :DH], s[:, DH:])

        if (c + 1) * SEG_CHUNK <= TOPK * TS:
            sort_chunk()
        else:
            pl.when(c * SEG_CHUNK < rows_used)(sort_chunk)

    def make_copy(tile_row, global_row, size):
        return pltpu.make_async_copy(s_ref.at[buf, _aligned(tile_row, size), :],
                                     xb_ref.at[_aligned(global_row, size), :], sem.at[buf])

    _segment_copies(tile, n8_ref, seg_ref, toff_ref, base_ref, make_copy)

    @pl.when(tile == NTS - 1)
    def _():
        wait_tile(tile - 1)
        wait_tile(tile)


def _dispatch(meta, base8, cnt8, hb, slot):
    n8, seg, toff = meta
    return pl.pallas_call(
        _dispatch_kernel,
        grid_spec=pltpu.PrefetchScalarGridSpec(
            num_scalar_prefetch=5,
            grid=(NTS,),
            in_specs=[pl.BlockSpec((TS, D), lambda i, *_: (i, 0)),
                      pl.BlockSpec((TOPK, TS), lambda i, *_: (0, i))],
            out_specs=pl.BlockSpec(memory_space=pl.ANY),
            scratch_shapes=[pltpu.VMEM((2, SEG, DH), U32), pltpu.VMEM((BLK, DH), U32),
                            pltpu.SemaphoreType.DMA((2,)), pltpu.SemaphoreType.DMA],
        ),
        out_shape=jax.ShapeDtypeStruct((NROWS, DH), U32),
        compiler_params=_cp("arbitrary"),
        name="dispatch",
    )(n8, seg, toff, base8, cnt8, hb, slot)


def _experts_kernel(layer, base_ref, nblk_ref, xb_hbm, w1_hbm, w3_hbm, w2_hbm, yb_hbm,
                    w1s_ref, w3s_ref, w2s_ref, w1b_ref, w3b_ref, w2b_ref, xin_ref, yout_ref,
                    wsem, isem, osem):
    e = pl.program_id(0)
    nblk = nblk_ref[e]
    first_row = base_ref[e]

    def fetch(expert):
        return [pltpu.make_async_copy(hbm.at[layer, expert], stage, wsem.at[j])
                for j, (hbm, stage) in enumerate(((w1_hbm, w1s_ref), (w3_hbm, w3s_ref), (w2_hbm, w2s_ref)))]

    def block_rows(j):
        return pl.ds(pl.multiple_of(first_row + j * BLK, BLK), BLK)

    def rows_in(j):
        return pltpu.make_async_copy(xb_hbm.at[block_rows(j), :], xin_ref.at[j % 2], isem.at[j % 2])

    def rows_out(j):
        return pltpu.make_async_copy(yout_ref.at[j % 2], yb_hbm.at[block_rows(j), :], osem.at[j % 2])

    @pl.when(e == 0)
    def _():
        for cp in fetch(e):
            cp.start()

    @pl.when(nblk > 0)
    def _():
        rows_in(0).start()

    for cp in fetch(e):
        cp.wait()

    def cast_chunk(j, carry):
        up_rows = pl.ds(pl.multiple_of(j * (D // CAST_CHUNKS), D // CAST_CHUNKS), D // CAST_CHUNKS)
        down_rows = pl.ds(pl.multiple_of(j * (DE // CAST_CHUNKS), DE // CAST_CHUNKS), DE // CAST_CHUNKS)
        w1b_ref[up_rows, :] = w1s_ref[up_rows, :].astype(BF16)
        w3b_ref[up_rows, :] = w3s_ref[up_rows, :].astype(BF16)
        w2b_ref[down_rows, :] = w2s_ref[down_rows, :].astype(BF16)
        return carry

    lax.fori_loop(0, CAST_CHUNKS, cast_chunk, 0)

    @pl.when(e + 1 < NE)
    def _():
        for cp in fetch(e + 1):
            cp.start()

    def block(j, carry):
        buf = j % 2
        rows_in(j).wait()

        @pl.when(j + 1 < nblk)
        def _():
            rows_in(j + 1).start()

        @pl.when(j >= 2)
        def _():
            rows_out(j - 2).wait()

        x_lo, x_hi = _unpack_pair(xin_ref[buf])

        def up(w_ref):
            return (jnp.dot(x_lo, w_ref[:DH, :], preferred_element_type=F32)
                    + jnp.dot(x_hi, w_ref[DH:, :], preferred_element_type=F32))

        hid = (_silu(up(w1b_ref)) * up(w3b_ref)).astype(BF16)
        for c in range(DH // MXU_N):
            cols = slice(c * MXU_N, (c + 1) * MXU_N)
            y_lo = jnp.dot(hid, w2b_ref[:, c * MXU_N:(c + 1) * MXU_N], preferred_element_type=F32)
            y_hi = jnp.dot(hid, w2b_ref[:, DH + c * MXU_N:DH + (c + 1) * MXU_N], preferred_element_type=F32)
            yout_ref[buf, :, cols] = _pack_pair(_round_bf16(y_lo), _round_bf16(y_hi))
        rows_out(j).start()
        return carry

    lax.fori_loop(0, nblk, block, 0)

    @pl.when(nblk >= 2)
    def _():
        rows_out(nblk - 2).wait()

    @pl.when(nblk >= 1)
    def _():
        rows_out(nblk - 1).wait()


def _experts(base8, nblk, xb, w1, w3, w2, layer):
    hbm = pl.BlockSpec(memory_space=pl.ANY)
    return pl.pallas_call(
        functools.partial(_experts_kernel, layer),
        grid_spec=pltpu.PrefetchScalarGridSpec(
            num_scalar_prefetch=2,
            grid=(NE,),
            in_specs=[hbm, hbm, hbm, hbm],
            out_specs=hbm,
            scratch_shapes=[pltpu.VMEM((D, DE), F32), pltpu.VMEM((D, DE), F32), pltpu.VMEM((DE, D), F32),
                            pltpu.VMEM((D, DE), BF16), pltpu.VMEM((D, DE), BF16), pltpu.VMEM((DE, D), BF16),
                            pltpu.VMEM((2, BLK, DH), U32), pltpu.VMEM((2, BLK, DH), U32),
                            pltpu.SemaphoreType.DMA((3,)), pltpu.SemaphoreType.DMA((2,)),
                            pltpu.SemaphoreType.DMA((2,))],
        ),
        out_shape=jax.ShapeDtypeStruct((NROWS, DH), U32),
        compiler_params=_cp("arbitrary"),
        name="experts",
    )(base8, nblk, xb, w1, w3, w2)


def _shared_kernel(h_ref, w1_ref, w3_ref, w2_ref, o_ref):
    x = h_ref[...].astype(BF16)
    hid = (_silu(jnp.dot(x, w1_ref[...], preferred_element_type=F32))
           * jnp.dot(x, w3_ref[...], preferred_element_type=F32))
    o_ref[...] = jnp.dot(hid.astype(BF16), w2_ref[...], preferred_element_type=F32)


def _shared(h2, ws1, ws3, ws2, layer):
    return pl.pallas_call(
        _shared_kernel,
        grid=(NT,),
        in_specs=[pl.BlockSpec((TM, D), lambda i: (i, 0)),
                  _layer_spec((D, DE), layer), _layer_spec((D, DE), layer), _layer_spec((DE, D), layer)],
        out_specs=pl.BlockSpec((TM, D), lambda i: (i, 0)),
        out_shape=jax.ShapeDtypeStruct((R, D), F32),
        compiler_params=_cp("arbitrary"),
        name="shared",
    )(h2, ws1, ws3, ws2)


def _combine_kernel(n8_ref, seg_ref, toff_ref, base_ref, x_ref, sh_ref, slot_ref, wt_ref, gp_ref, gs_ref,
                    gpost_ref, yb_ref, o_ref, ys_ref, f_ref, cm_ref, sem):
    tile = pl.program_id(0)
    buf = tile % 2

    def fetch_tile(t):
        def make_copy(tile_row, global_row, size):
            return pltpu.make_async_copy(yb_ref.at[_aligned(global_row, size), :],
                                         ys_ref.at[t % 2, _aligned(tile_row, size), :], sem.at[t % 2])

        _segment_copies(t, n8_ref, seg_ref, toff_ref, base_ref, make_copy)

    @pl.when(tile == 0)
    def _():
        ys_ref[...] = jnp.zeros_like(ys_ref)
        fetch_tile(tile)

    @pl.when(tile + 1 < NTS)
    def _():
        fetch_tile(tile + 1)

    cols = lax.broadcasted_iota(I32, (TS, SEG), 1)
    cmat = jnp.zeros((TS, SEG), F32)
    for k in range(TOPK):
        cmat = jnp.where(cols == slot_ref[:, k:k + 1], wt_ref[:, k:k + 1], cmat)
    cm_ref[...] = cmat.astype(BF16)

    rows = _tile_rows(tile, n8_ref, seg_ref)
    pltpu.make_async_copy(yb_ref.at[_aligned(0, rows), :], ys_ref.at[buf, _aligned(0, rows), :],
                          sem.at[buf]).wait()

    def weighted(ks):
        y_lo, y_hi = _unpack_pair(ys_ref[buf, ks, :])
        return (jnp.dot(cm_ref[:, ks], y_lo, preferred_element_type=F32),
                jnp.dot(cm_ref[:, ks], y_hi, preferred_element_type=F32))

    lo, hi = weighted(slice(0, TOPK * TS))
    f_ref[:, :DH] = sh_ref[:, :DH] + lo
    f_ref[:, DH:] = sh_ref[:, DH:] + hi
    for start in range(TOPK * TS, SEG, COMB_CHUNK):
        @pl.when(start < rows)
        def _(start=start):
            lo, hi = weighted(slice(start, start + COMB_CHUNK))
            f_ref[:, :DH] += lo
            f_ref[:, DH:] += hi
    gate = jnp.where(tile < RP // TS, gp_ref[...], gs_ref[...])
    gpost = gpost_ref[...]
    for s in range(TS // SUB):
        rows = slice(s * SUB, (s + 1) * SUB)
        o_ref[rows, :] = x_ref[rows, :] + gate * (_rms(f_ref[rows, :]) * gpost)


def _combine(meta, base8, x, shared, slot_t, w_t, ada_p, ada_s, gpost, yb):
    n8, seg, toff = meta
    tiles_per_seq = LP // TS
    gp = pl.BlockSpec((None, 1, D), lambda i, *_: (jnp.minimum(i // tiles_per_seq, NP - 1), 0, 5))
    gs = pl.BlockSpec((NS, D), lambda i, *_: (0, 5))
    row = pl.BlockSpec((TS, D), lambda i, *_: (i, 0))
    tok = pl.BlockSpec((TS, TOPK), lambda i, *_: (i, 0))
    return pl.pallas_call(
        _combine_kernel,
        grid_spec=pltpu.PrefetchScalarGridSpec(
            num_scalar_prefetch=4,
            grid=(NTS,),
            in_specs=[row, row, tok, tok, gp, gs, pl.BlockSpec((1, D), lambda i, *_: (0, 0)),
                      pl.BlockSpec(memory_space=pl.ANY)],
            out_specs=row,
            scratch_shapes=[pltpu.VMEM((2, SEG, DH), U32), pltpu.VMEM((TS, D), F32),
                            pltpu.VMEM((TS, SEG), BF16),
                            pltpu.SemaphoreType.DMA((2,))],
        ),
        out_shape=jax.ShapeDtypeStruct((R, D), F32),
        compiler_params=_cp("arbitrary"),
        name="combine",
    )(n8, seg, toff, base8, x, shared, slot_t, w_t, ada_p, ada_s, gpost, yb)


def _row(v):
    return v.reshape(1, -1)


def _to_rows(x_prompt, x_sample):
    return jnp.concatenate([x_prompt.reshape(RP, D), x_sample.transpose(1, 0, 2).reshape(RS, D)], axis=0)


def _sample_rows_to_batch(rows, width):
    return rows.reshape(LS, NS, width).transpose(1, 0, 2)


def kernel(x_prompt, x_sample, state_conv, state_pool, c_prompt, c_sample, w_ada, b_ada, g_pre_mix,
           g_post_mix, g_pre_ffn, g_post_ffn, w_in, conv_w, conv_b, conv_ln_g, conv_ln_b, conv_w_pw,
           gmlp_ln_g, gmlp_ln_b, gmlp_w_s, gmlp_b_s, gmlp_w_o, pool_w, pool_scale, w_out, router_w,
           router_bias, w1, w3, w2, ws1, ws3, ws2):
    x = _to_rows(x_prompt, x_sample)
    ada = _ada(jnp.concatenate([c_prompt, c_sample], axis=0), w_ada, b_ada)
    wpw_b, wo_b, wpool_b, wout_b = (w.astype(BF16) for w in (conv_w_pw, gmlp_w_o, pool_w, w_out))
    ws1_b, ws3_b, ws2_b = (w.astype(BF16) for w in (ws1, ws3, ws2))
    conv_p, pool_p, conv_s, pool_s, v_s = [], [], [], [], []
    for l in range(DEPTH):
        ada_p = ada[l, :NP].reshape(NP, 1, 6 * D)
        ada_s = ada[l, NP:]

        h = _hmod(x, _row(g_pre_mix[l]), ada_p, ada_s, 1, 0)
        glu = _colmm(h, w_in, l, (0, DC), DC, 512, _glu_epilogue, F32, "in_glu")
        zb = _colmm(h, w_in, l, (COL_GMLP,), 2 * DC, 1024, _gelu_epilogue, F32, "in_gelu")
        p = _colmm(h, w_in, l, (COL_POOL,), DC, 1024, _id_epilogue, F32, "in_pool")
        gates = _colmm(h, w_in, l, (COL_GATE,), 3 * D, 1024, _sigmoid_epilogue, BF16, "in_gate")

        cw, cb = conv_w[l], _row(conv_b[l])
        clg, clb = _row(conv_ln_g[l]), _row(conv_ln_b[l])
        a_p = _conv_p(glu, cw, cb, clg, clb)
        a_s = _conv_s(glu, state_conv[l].transpose(1, 0, 2), cw, cb, clg, clb)

        glg, glb = _row(gmlp_ln_g[l]), _row(gmlp_ln_b[l])
        b_p = _gmlp_p(zb, gmlp_w_s[l], gmlp_b_s[l].T, glg, glb)
        coef = jnp.repeat(gmlp_w_s[l][:, :LS, :LS].transpose(1, 2, 0).reshape(LS * LS, HEADS), DC // HEADS, axis=1)
        bias = jnp.repeat(gmlp_b_s[l][:, :LS].T, DC // HEADS, axis=1)
        b_s, v_rows = _gmlp_s(zb, coef, bias, glg, glb)

        c_p = _pool_p(p)
        c_s = _pool_s(p, state_pool[l].transpose(1, 0, 2))

        x = _merge(((a_p, a_s), (b_p, b_s), (c_p, c_s)), gates, x, ada_p, ada_s, _row(g_post_mix[l]),
                   wpw_b, wo_b, wpool_b, _row(pool_scale[l]), wout_b, l)

        conv_p.append(jnp.stack([glu[(b + 1) * LP - HIST_C:(b + 1) * LP] for b in range(NP)]))
        pool_p.append(jnp.stack([p[(b + 1) * LP - HIST_P:(b + 1) * LP] for b in range(NP)]))
        conv_s.append(jnp.concatenate([state_conv[l][:, LS:], _sample_rows_to_batch(glu[RP:], DC)], axis=1))
        pool_s.append(jnp.concatenate([state_pool[l][:, LS:], _sample_rows_to_batch(p[RP:], DC)], axis=1))
        v_s.append(_sample_rows_to_batch(v_rows, DC))

        wr_t = router_w[l].T
        wr_hi = wr_t.astype(BF16)
        wr_lo = (wr_t - wr_hi.astype(F32)).astype(BF16)
        hb, slot, w_k, meta, cnt = _route(x, _row(g_pre_ffn[l]), ada_p, ada_s, wr_hi, wr_lo,
                                          router_bias[l].reshape(NE, 1))
        meta = tuple(meta[:, c] for c in range(3))
        cnt8 = cnt[:, 0]
        padded = (cnt8 + BLK - 1) // BLK * BLK
        pad_end = jnp.cumsum(padded)
        base8 = (pad_end - padded).astype(I32)

        xb = _dispatch(meta, base8, cnt8, hb, slot)
        yb = _experts(base8, (padded // BLK).astype(I32), xb, w1, w3, w2, l)
        shared = _shared(hb, ws1_b, ws3_b, ws2_b, l)
        x = _combine(meta, base8, x, shared, slot.T, w_k.T, ada_p, ada_s, _row(g_post_ffn[l]), yb)

    y_prompt = x[:RP].reshape(NP, LP, D)
    y_sample = _sample_rows_to_batch(x[RP:], D)
    return (y_prompt, y_sample, jnp.stack(conv_p), jnp.stack(pool_p), jnp.stack(conv_s),
            jnp.stack(pool_s), jnp.stack(v_s))
```
